```python
import jax, jax.numpy as jnp
from jax import lax
import numpy as np

D_MODEL = 1024
BATCH = 8
SEQ = 4096
DEPTH = 2

HEAD_DIM = 64
RW_HEADS = 8
RW_WIDTH = RW_HEADS * HEAD_DIM
RW_DECAY_LORA = 64
RW_A_LORA = 64
RW_V_LORA = 32
RW_G_LORA = 128
RW_GN_EPS = 64e-5
FOX_HEADS = 8
FOX_WIDTH = FOX_HEADS * HEAD_DIM
MLA_HEADS = 8
MLA_NOPE = 64
MLA_ROPE = 32
MLA_V = 64
MLA_Q_LORA = 256
MLA_KV_LORA = 256
ROPE_BASE = 10000.0
N_BRANCH = 3
BRANCH_WIDTH = 512
D_FF = 2816
CONV_WIDTH = 3
Q_BLOCK = 128
NORM_EPS = 1e-6

RW_SIZES = (RW_WIDTH, RW_DECAY_LORA, RW_WIDTH, RW_WIDTH, RW_A_LORA, RW_G_LORA)
RW_COLS = 3 * RW_WIDTH + RW_DECAY_LORA + RW_A_LORA + RW_G_LORA
REST_SIZES = (FOX_WIDTH, FOX_WIDTH, FOX_WIDTH, FOX_HEADS,
              MLA_Q_LORA, MLA_KV_LORA, MLA_ROPE, N_BRANCH * D_MODEL)
N_IN_COLS = RW_COLS + 3 * FOX_WIDTH + FOX_HEADS + MLA_Q_LORA + MLA_KV_LORA + MLA_ROPE + N_BRANCH * D_MODEL

kernel_name = 'hybrid_rwkv7_fox_mla_convffn_adaln'


def rms_norm(x, g, eps=NORM_EPS):
    xf = x.astype(jnp.float32)
    y = xf * lax.rsqrt(jnp.mean(xf * xf, axis=-1, keepdims=True) + eps)
    return (y * g.astype(jnp.float32)).astype(x.dtype)


def split_cols(p, sizes):
    idx, acc = [], 0
    for s in sizes[:-1]:
        acc += s
        idx.append(acc)
    return jnp.split(p, idx, axis=-1)


def token_shift_mix(p, mu):
    prev = jnp.pad(p, ((0, 0), (1, 0), (0, 0)))[:, :-1]
    return p + (prev - p) * mu


def apply_rope(t, positions):
    half = t.shape[-1] // 2
    inv_freq = jnp.power(ROPE_BASE, -jnp.arange(half, dtype=jnp.float32) / half)
    ang = positions.astype(jnp.float32)[:, :, None, None] * inv_freq
    cos, sin = jnp.cos(ang), jnp.sin(ang)
    tf = t.astype(jnp.float32)
    t1, t2 = tf[..., :half], tf[..., half:]
    return jnp.concatenate([t1 * cos - t2 * sin, t1 * sin + t2 * cos], axis=-1).astype(t.dtype)


def causal_block_attention(q, k, v, scale, log_f_cum=None):
    T = q.shape[2]
    outs = []
    for blk in range(T // Q_BLOCK):
        q0 = blk * Q_BLOCK
        L = q0 + Q_BLOCK
        s = jnp.einsum('bhqd,bhkd->bhqk', q[:, :, q0:L], k[:, :, :L]).astype(jnp.float32) * scale
        if log_f_cum is not None:
            s = s + log_f_cum[:, :, q0:L, None] - log_f_cum[:, :, None, :L]
        causal = (q0 + jnp.arange(Q_BLOCK))[:, None] >= jnp.arange(L)[None, :]
        s = jnp.where(causal, s, -jnp.inf)
        p = jax.nn.softmax(s, axis=-1).astype(v.dtype)
        outs.append(jnp.einsum('bhqk,bhkd->bhqd', p, v[:, :, :L]))
    return jnp.concatenate(outs, axis=2)


def rwkv7_time_mix(r, wl, k, v, al, gl, w_up, w0, a_up, a0, g_up, k_k, k_a, r_k, gn_w, gn_b,
                   v_first, vres):
    B, T, _ = r.shape
    H, N = RW_HEADS, HEAD_DIM
    f32 = jnp.float32
    w = -jax.nn.softplus(-(w0 + jnp.tanh(wl) @ w_up).astype(f32)) - 0.5
    decay = jnp.exp(-jnp.exp(w))
    a = jax.nn.sigmoid(a0 + al @ a_up)
    g = jax.nn.sigmoid(gl) @ g_up
    if vres is not None:
        lv, v_up, v0 = vres
        v = v + (v_first - v) * jax.nn.sigmoid(v0 + lv @ v_up)
    kk = (k * k_k).reshape(B, T, H, N).astype(f32)
    kk = kk * lax.rsqrt(jnp.maximum(jnp.sum(kk * kk, axis=-1, keepdims=True), 1e-24))
    k = k * (1 + (a - 1) * k_a)

    def heads_tm(t):
        return jnp.moveaxis(t.reshape(B, T, H, N).astype(f32), 1, 0)

    xs = (heads_tm(r), heads_tm(decay), heads_tm(k), heads_tm(v), jnp.moveaxis(kk, 1, 0), heads_tm(a))

    def step(S, inp):
        r_t, d_t, k_t, v_t, kk_t, a_t = inp
        s_kk = jnp.einsum('bhvk,bhk->bhv', S, kk_t)
        S = (S * d_t[:, :, None, :]
             - s_kk[..., None] * (kk_t * a_t)[:, :, None, :]
             + v_t[..., None] * k_t[:, :, None, :])
        return S, jnp.einsum('bhvk,bhk->bhv', S, r_t)

    _, y = lax.scan(step, jnp.zeros((B, H, N, N), f32), xs)
    y = jnp.moveaxis(y, 0, 1)
    mu = jnp.mean(y, axis=-1, keepdims=True)
    var = jnp.mean(jnp.square(y - mu), axis=-1, keepdims=True)
    y = (y - mu) * lax.rsqrt(var + RW_GN_EPS)
    y = y * gn_w.reshape(H, N).astype(f32) + gn_b.reshape(H, N).astype(f32)
    bonus = jnp.sum((r * k * r_k).reshape(B, T, H, N).astype(f32), axis=-1, keepdims=True)
    y = y + bonus * v.reshape(B, T, H, N).astype(f32)
    return y.reshape(B, T, H * N).astype(r.dtype) * g, v


def fox_attention(q, k, v, f_logit, b_f):
    B, T, _ = q.shape

    def heads(t):
        return t.reshape(B, T, FOX_HEADS, HEAD_DIM).transpose(0, 2, 1, 3)

    log_f = jax.nn.log_sigmoid((f_logit + b_f).astype(jnp.float32))
    cum = jnp.cumsum(log_f, axis=1).transpose(0, 2, 1)
    o = causal_block_attention(heads(q), heads(k), heads(v), HEAD_DIM ** -0.5, cum)
    return o.transpose(0, 2, 1, 3).reshape(B, T, FOX_WIDTH)


def mla_attention(q_lat, kv_lat, k_rope, positions, q_norm_g, w_q_up, kv_norm_g, w_kv_up):
    B, T, _ = q_lat.shape
    H = MLA_HEADS
    q = (rms_norm(q_lat, q_norm_g) @ w_q_up).reshape(B, T, H, MLA_NOPE + MLA_ROPE)
    kv = (rms_norm(kv_lat, kv_norm_g) @ w_kv_up).reshape(B, T, H, MLA_NOPE + MLA_V)
    q_nope, q_pe = q[..., :MLA_NOPE], apply_rope(q[..., MLA_NOPE:], positions)
    k_nope, v = kv[..., :MLA_NOPE], kv[..., MLA_NOPE:]
    k_pe = jnp.broadcast_to(apply_rope(k_rope[:, :, None, :], positions), (B, T, H, MLA_ROPE))
    q = jnp.concatenate([q_nope, q_pe], axis=-1).transpose(0, 2, 1, 3)
    k = jnp.concatenate([k_nope, k_pe], axis=-1).transpose(0, 2, 1, 3)
    o = causal_block_attention(q, k, v.transpose(0, 2, 1, 3), (MLA_NOPE + MLA_ROPE) ** -0.5)
    return o.transpose(0, 2, 1, 3).reshape(B, T, H * MLA_V)


def conv_ffn(h, w_up, conv_w, conv_b, w_down):
    u = h @ w_up
    C = u.shape[-1]
    u = lax.conv_general_dilated(u, conv_w[:, None, :], window_strides=(1,),
                                 padding=((CONV_WIDTH - 1, 0),),
                                 dimension_numbers=('NWC', 'WIO', 'NWC'),
                                 feature_group_count=C) + conv_b
    ug, uv = jnp.split(u, 2, axis=-1)
    return (jax.nn.silu(ug) * uv) @ w_down


def setup_inputs(seed: int = 0) -> dict:
    key = jax.random.key(seed)
    ks = iter(jax.random.split(key, 48))
    L, D = DEPTH, D_MODEL

    def nrm(shape, scale):
        return jax.random.normal(next(ks), shape, jnp.float32) * scale

    def unif(shape, lo, hi):
        return jax.random.uniform(next(ks), shape, jnp.float32, lo, hi)

    x = nrm((BATCH, SEQ, D), 1.0)
    c = nrm((BATCH, D), 1.0)
    offset = jax.random.randint(next(ks), (BATCH, 1), 0, 2048, jnp.int32)
    positions = offset + jnp.arange(SEQ, dtype=jnp.int32)[None, :]
    return {
        'x': x, 'c': c, 'positions': positions,
        'norm1_g': 1.0 + nrm((L, D), 0.02),
        'norm2_g': 1.0 + nrm((L, D), 0.02),
        'w_ada': nrm((L, D, 6 * D), 0.5 * D ** -0.5),
        'b_ada': nrm((L, 6 * D), 0.02),
        'w_in': nrm((L, D, N_IN_COLS), D ** -0.5),
        'mu_shift': unif((L, RW_COLS), 0.0, 1.0),
        'rw_w_up': nrm((L, RW_DECAY_LORA, RW_WIDTH), 0.1 * RW_DECAY_LORA ** -0.5),
        'rw_w0': unif((L, RW_WIDTH), -6.0, -1.0),
        'rw_a_up': nrm((L, RW_A_LORA, RW_WIDTH), RW_A_LORA ** -0.5),
        'rw_a0': nrm((L, RW_WIDTH), 0.1),
        'rw_g_up': nrm((L, RW_G_LORA, RW_WIDTH), RW_G_LORA ** -0.5),
        'rw_k_k': 0.85 + nrm((L, RW_WIDTH), 0.05),
        'rw_k_a': 1.0 + nrm((L, RW_WIDTH), 0.05),
        'rw_r_k': nrm((L, RW_WIDTH), 0.1),
        'rw_gn_w': 1.0 + nrm((L, RW_WIDTH), 0.02),
        'rw_gn_b': nrm((L, RW_WIDTH), 0.02),
        'rw_vres_down': nrm((L - 1, D, RW_V_LORA), D ** -0.5),
        'rw_vres_mu': unif((L - 1, RW_V_LORA), 0.0, 1.0),
        'rw_vres_up': nrm((L - 1, RW_V_LORA, RW_WIDTH), RW_V_LORA ** -0.5),
        'rw_v0': nrm((L - 1, RW_WIDTH), 0.1),
        'fox_b_f': unif((L, FOX_HEADS), 1.0, 4.0),
        'mla_q_norm_g': 1.0 + nrm((L, MLA_Q_LORA), 0.02),
        'mla_w_q_up': nrm((L, MLA_Q_LORA, MLA_HEADS * (MLA_NOPE + MLA_ROPE)), MLA_Q_LORA ** -0.5),
        'mla_kv_norm_g': 1.0 + nrm((L, MLA_KV_LORA), 0.02),
        'mla_w_kv_up': nrm((L, MLA_KV_LORA, MLA_HEADS * (MLA_NOPE + MLA_V)), MLA_KV_LORA ** -0.5),
        'w_branch': nrm((L, N_BRANCH, BRANCH_WIDTH, D), BRANCH_WIDTH ** -0.5),
        'w_o': nrm((L, D, D), D ** -0.5),
        'ffn_w_up': nrm((L, D, 2 * D_FF), D ** -0.5),
        'ffn_conv_w': nrm((L, CONV_WIDTH, 2 * D_FF), CONV_WIDTH ** -0.5),
        'ffn_conv_b': nrm((L, 2 * D_FF), 0.02),
        'ffn_w_down': nrm((L, D_FF, D), D_FF ** -0.5),
        'final_g': 1.0 + nrm((D,), 0.02),
    }


def reference(x, c, positions, norm1_g, norm2_g, w_ada, b_ada, w_in, mu_shift,
              rw_w_up, rw_w0, rw_a_up, rw_a0, rw_g_up, rw_k_k, rw_k_a, rw_r_k, rw_gn_w, rw_gn_b,
              rw_vres_down, rw_vres_mu, rw_vres_up, rw_v0, fox_b_f,
              mla_q_norm_g, mla_w_q_up, mla_kv_norm_g, mla_w_kv_up,
              w_branch, w_o, ffn_w_up, ffn_conv_w, ffn_conv_b, ffn_w_down, final_g):
    B, T, D = x.shape
    c_act = jax.nn.silu(c)
    v_first = None
    for i in range(DEPTH):
        mod = (c_act @ w_ada[i] + b_ada[i])[:, None, :]
        sh1, sc1, g1, sh2, sc2, g2 = jnp.split(mod, 6, axis=-1)

        h = rms_norm(x, norm1_g[i]) * (1 + sc1) + sh1
        w_cat = w_in[i] if i == 0 else jnp.concatenate([w_in[i], rw_vres_down[i - 1]], axis=1)
        proj = h @ w_cat
        rw_part = token_shift_mix(proj[..., :RW_COLS], mu_shift[i])
        r, wl, k, v, al, gl = split_cols(rw_part, RW_SIZES)
        fq, fk, fv, ff, mq, mkv, mkr, gate_logits = split_cols(proj[..., RW_COLS:N_IN_COLS], REST_SIZES)
        if i == 0:
            vres = None
        else:
            vres = (token_shift_mix(proj[..., N_IN_COLS:], rw_vres_mu[i - 1]), rw_vres_up[i - 1], rw_v0[i - 1])

        y_rw, v_rw = rwkv7_time_mix(r, wl, k, v, al, gl, rw_w_up[i], rw_w0[i], rw_a_up[i], rw_a0[i],
                                    rw_g_up[i], rw_k_k[i], rw_k_a[i], rw_r_k[i], rw_gn_w[i], rw_gn_b[i],
                                    v_first, vres)
        if i == 0:
            v_first = v_rw
        y_fox = fox_attention(fq, fk, fv, ff, fox_b_f[i])
        y_mla = mla_attention(mq, mkv, mkr, positions, mla_q_norm_g[i], mla_w_q_up[i],
                              mla_kv_norm_g[i], mla_w_kv_up[i])

        branches = jnp.stack([y_rw, y_fox, y_mla], axis=2)
        branch_d = jnp.einsum('btgc,gcd->btgd', branches, w_branch[i])
        gates = jax.nn.sigmoid(gate_logits).reshape(B, T, N_BRANCH, D)
        mixed = jnp.sum(gates * branch_d, axis=2) @ w_o[i]
        x = x + g1 * mixed

        h = rms_norm(x, norm2_g[i]) * (1 + sc2) + sh2
        x = x + g2 * conv_ffn(h, ffn_w_up[i], ffn_conv_w[i], ffn_conv_b[i], ffn_w_down[i])
    return rms_norm(x, final_g)
```

```python
import functools

import numpy as np
import jax
import jax.numpy as jnp
from jax import lax
from jax.experimental import pallas as pl
from jax.experimental.pallas import tpu as pltpu

F32 = jnp.float32
BF16 = jnp.bfloat16
HIGHEST = lax.Precision.HIGHEST

D_MODEL = 1024
HEAD_DIM = 64
N_HEADS = 8
WIDTH = N_HEADS * HEAD_DIM
RW_DECAY_LORA = 64
RW_A_LORA = 64
RW_V_LORA = 32
RW_G_LORA = 128
RW_GN_EPS = 64e-5
MLA_NOPE = 64
MLA_ROPE = 32
MLA_QK = MLA_NOPE + MLA_ROPE
MLA_LORA = 256
ROPE_BASE = 10000.0
N_BRANCH = 3
D_FF = 2816
NORM_EPS = 1e-6
NEG_BIG = -1e30

VMEM_LIMIT_BYTES = 56 * 1024 * 1024

C_RKV = 0
C_SMALL = 3 * WIDTH
SMALL_W = 384
S_WL, S_AL, S_GL, S_VR, S_KR, S_FF = 0, 64, 128, 256, 288, 320
C_FOX = C_SMALL + SMALL_W
C_LAT = C_FOX + 3 * WIDTH
C_GATE = C_LAT + 2 * MLA_LORA
N_COLS = C_GATE + N_BRANCH * D_MODEL
N_SHIFT = C_FOX

RW_CHUNK = 64
RW_GROUP = 4
GW = RW_GROUP * HEAD_DIM


def _cparams(sem):
    return pltpu.CompilerParams(dimension_semantics=sem, vmem_limit_bytes=VMEM_LIMIT_BYTES)


def _const_spec(shape):
    n = len(shape)
    return pl.BlockSpec(shape, lambda *_: (0,) * n, pipeline_mode=pl.Buffered(1))


def _sigmoid(z):
    return 1.0 / (1.0 + jnp.exp(-z))


def _softplus(z):
    return jnp.maximum(z, 0.0) + jnp.log1p(jnp.exp(-jnp.abs(z)))


def _modulated_norm(x, gain, shift, scale):
    ms = jnp.mean(x * x, axis=-1, keepdims=True)
    return (x * lax.rsqrt(ms + NORM_EPS) * gain) * (1.0 + scale) + shift


def _ada_body(c_ref, w_ref, b_ref, o_ref):
    c = c_ref[...]
    ca = c * _sigmoid(c)
    o_ref[0] = jnp.dot(ca, w_ref[0], precision=HIGHEST, preferred_element_type=F32) + b_ref[0]


def _ada(c, w_ada, b_ada):
    L, D, N6 = w_ada.shape
    B = c.shape[0]
    tn = 1536
    return pl.pallas_call(
        _ada_body,
        out_shape=jax.ShapeDtypeStruct((L, B, N6), F32),
        grid=(L, N6 // tn),
        in_specs=[pl.BlockSpec((B, D), lambda l, n: (0, 0)),
                  pl.BlockSpec((1, D, tn), lambda l, n: (l, 0, n)),
                  pl.BlockSpec((1, 1, tn), lambda l, n: (l, 0, n))],
        out_specs=pl.BlockSpec((1, B, tn), lambda l, n: (l, 0, n)),
        compiler_params=_cparams(("arbitrary", "arbitrary")),
        name="ada",
    )(c, w_ada, b_ada.reshape(L, 1, N6))


def _proj_body(x_ref, mod_ref, g_ref, w_ref, wff_ref, mu_ref, bf_ref,
               rkv_ref, small_ref, fq_ref, fk_ref, fv_ref, lat_ref, gate_ref, cumr_ref,
               carry_ref, cc_row_ref, *, tm):
    j = pl.program_id(1)

    @pl.when(j == 0)
    def _():
        carry_ref[...] = jnp.zeros_like(carry_ref)
        cc_row_ref[...] = jnp.zeros_like(cc_row_ref)

    h = _modulated_norm(x_ref[0], g_ref[...], mod_ref[0, 0:1, :], mod_ref[0, 1:2, :])
    hb = h.astype(BF16)
    row = lax.broadcasted_iota(jnp.int32, (tm, 1), 0)

    def mm(c0, c1):
        return jnp.dot(hb, w_ref[:, c0:c1], preferred_element_type=F32)

    def shifted(c0, c1):
        res = mm(c0, c1)
        last = carry_ref[0:1, c0:c1]
        carry_ref[0:1, c0:c1] = res[tm - 1:tm, :]
        prev = jnp.where(row == 0, last, pltpu.roll(res, 1, 0))
        return res + (prev - res) * mu_ref[:, c0:c1]

    for i in range(3):
        c0 = C_RKV + i * WIDTH
        rkv_ref[0, :, c0:c0 + WIDTH] = shifted(c0, c0 + WIDTH).astype(BF16)
    small_ref[0] = shifted(C_SMALL, C_SMALL + SMALL_W)

    for i, (o_ref, scale) in enumerate(((fq_ref, HEAD_DIM ** -0.5), (fk_ref, None), (fv_ref, None))):
        res = mm(C_FOX + i * WIDTH, C_FOX + (i + 1) * WIDTH)
        if scale is not None:
            res = res * scale
        for hd in range(N_HEADS):
            o_ref[0, hd] = res[:, hd * HEAD_DIM:(hd + 1) * HEAD_DIM].astype(BF16)

    lat_ref[0] = mm(C_LAT, C_GATE)

    for i in range(N_BRANCH * D_MODEL // WIDTH):
        c0 = C_GATE + i * WIDTH
        gate_ref[0, :, i * WIDTH:(i + 1) * WIDTH] = _sigmoid(mm(c0, c0 + WIDTH)).astype(BF16)

    ff_t = lax.dot_general(wff_ref[...], hb, (((1,), (1,)), ((), ())), preferred_element_type=F32)
    z = ff_t + bf_ref[...]
    logf = jnp.minimum(z, 0.0) - jnp.log1p(jnp.exp(-jnp.abs(z)))
    ri = lax.broadcasted_iota(jnp.int32, (tm, tm), 0)
    ci = lax.broadcasted_iota(jnp.int32, (tm, tm), 1)
    tri_u = (ri <= ci).astype(F32)
    cum_r = jnp.dot(logf, tri_u, precision=HIGHEST, preferred_element_type=F32) + cc_row_ref[:, 0:1]
    cumr_ref[0] = cum_r
    cc_row_ref[:, 0:1] = cum_r[:, tm - 1:tm]


def _proj(x, mod, gain, w_cat, w_ff_t, mu, b_f, *, tm=256):
    B, T, D = x.shape
    nt = T // tm
    hm = jax.ShapeDtypeStruct((B, N_HEADS, T, HEAD_DIM), BF16)
    hm_spec = pl.BlockSpec((1, N_HEADS, tm, HEAD_DIM), lambda b, j: (b, 0, j, 0))
    row3 = lambda b, j: (b, j, 0)
    return pl.pallas_call(
        functools.partial(_proj_body, tm=tm),
        out_shape=(jax.ShapeDtypeStruct((B, T, 3 * WIDTH), BF16),
                   jax.ShapeDtypeStruct((B, T, SMALL_W), F32),
                   hm, hm, hm,
                   jax.ShapeDtypeStruct((B, T, 2 * MLA_LORA), F32),
                   jax.ShapeDtypeStruct((B, T, N_BRANCH * D), BF16),
                   jax.ShapeDtypeStruct((B, N_HEADS, T), F32)),
        grid=(B, nt),
        in_specs=[pl.BlockSpec((1, tm, D), row3),
                  pl.BlockSpec((1, 6, D), lambda b, j: (b, 0, 0)),
                  _const_spec((1, D)),
                  _const_spec((D, N_COLS)),
                  _const_spec((N_HEADS, D)),
                  _const_spec((1, N_SHIFT)),
                  _const_spec((N_HEADS, 1))],
        out_specs=(pl.BlockSpec((1, tm, 3 * WIDTH), row3),
                   pl.BlockSpec((1, tm, SMALL_W), row3),
                   hm_spec, hm_spec, hm_spec,
                   pl.BlockSpec((1, tm, 2 * MLA_LORA), row3),
                   pl.BlockSpec((1, tm, N_BRANCH * D), row3),
                   pl.BlockSpec((1, N_HEADS, tm), lambda b, j: (b, 0, j))),
        scratch_shapes=[pltpu.VMEM((8, N_SHIFT), F32),
                        pltpu.VMEM((N_HEADS, 128), F32)],
        compiler_params=_cparams(("arbitrary", "arbitrary")),
        name="proj",
    )(x, mod, gain, w_cat, w_ff_t, mu, b_f)


def _rwkv_body(*refs, has_vres):
    if has_vres:
        (rkv_ref, small_ref, vfirst_ref, lw_ref, la_ref, lg_ref, lv_ref, vec_ref, ones_ref,
         y_ref, ht_ref) = refs
    else:
        rkv_ref, small_ref, lw_ref, la_ref, lg_ref, vec_ref, ones_ref, y_ref, ht_ref = refs
    C = RW_CHUNK
    j = pl.program_id(1)

    @pl.when(j == 0)
    def _():
        ht_ref[...] = jnp.zeros_like(ht_ref)

    def vec(i):
        return vec_ref[i:i + 1, :]

    w0, a0, k_k, k_a, r_k, gn_w, gn_b, v0 = (vec(i) for i in range(8))
    ones_bd = ones_ref[...]

    def headsum(t):
        hi = t.astype(BF16)
        lo = (t - hi.astype(F32)).astype(BF16)
        return (jnp.dot(hi, ones_bd, preferred_element_type=F32)
                + jnp.dot(lo, ones_bd, preferred_element_type=F32))

    def lora(t, w_ref):
        return jnp.dot(t.astype(BF16), w_ref[...], preferred_element_type=F32)

    r = rkv_ref[0, :, 0:WIDTH].astype(F32)
    k = rkv_ref[0, :, WIDTH:2 * WIDTH].astype(F32)
    v = rkv_ref[0, :, 2 * WIDTH:3 * WIDTH].astype(F32)
    sm = small_ref[0]
    wl = sm[:, S_WL:S_WL + RW_DECAY_LORA]
    al = sm[:, S_AL:S_AL + RW_A_LORA]
    gl = sm[:, S_GL:S_GL + RW_G_LORA]

    w = -_softplus(-(w0 + lora(jnp.tanh(wl), lw_ref))) - 0.5
    logd = -jnp.exp(w)
    a = _sigmoid(a0 + lora(al, la_ref))
    g = lora(_sigmoid(gl), lg_ref)
    if has_vres:
        lvv = sm[:, S_VR:S_VR + RW_V_LORA]
        v = v + (vfirst_ref[0].astype(F32) - v) * _sigmoid(v0 + lora(lvv, lv_ref))
    kk = k * k_k
    kk = kk * lax.rsqrt(jnp.maximum(headsum(kk * kk), 1e-24))
    k2 = k * (1.0 + (a - 1.0) * k_a)

    ri = lax.broadcasted_iota(jnp.int32, (C, C), 0)
    ci = lax.broadcasted_iota(jnp.int32, (C, C), 1)
    cl = jnp.dot((ci <= ri).astype(F32), logd, precision=HIGHEST, preferred_element_type=F32)
    cl_last = cl[C - 1:C, :]
    e_neg = jnp.exp(-cl)
    e_end = jnp.exp(cl_last - cl)
    ka = kk * a
    a_t = kk * jnp.exp(cl - logd)
    r_t = r * jnp.exp(cl)
    b_m = -ka * e_neg
    k_m = k2 * e_neg
    b_p = -ka * e_end
    k_p = k2 * e_end
    p_c = jnp.exp(cl_last)

    bi = lax.broadcasted_iota(jnp.int32, (GW, GW), 0)
    bj = lax.broadcasted_iota(jnp.int32, (GW, GW), 1)
    bd_mask = (bi // HEAD_DIM) == (bj // HEAD_DIM)
    strict = bi > bj
    incl = bi >= bj
    eye = (bi == bj).astype(F32)

    def bd(t):
        return jnp.where(bd_mask, jnp.concatenate([t] * RW_GROUP, axis=0), 0.0)

    def collapse(t):
        return sum(t[hh * C:(hh + 1) * C, :] for hh in range(RW_GROUP))

    def dot16(p, q):
        return jnp.dot(p.astype(BF16), q.astype(BF16), preferred_element_type=F32)

    ys = []
    for gi in range(N_HEADS // RW_GROUP):
        sl = slice(gi * GW, (gi + 1) * GW)
        at_bd, rt_bd, bm_bd, km_bd = bd(a_t[:, sl]), bd(r_t[:, sl]), bd(b_m[:, sl]), bd(k_m[:, sl])
        bp_bd, kp_bd, v_bd = bd(b_p[:, sl]), bd(k_p[:, sl]), bd(v[:, sl])

        lhs = jnp.concatenate([at_bd, rt_bd], axis=0).astype(BF16)
        rhs = jnp.concatenate([bm_bd, km_bd], axis=0).astype(BF16)
        a_all = lax.dot_general(lhs, rhs, (((1,), (1,)), ((), ())), preferred_element_type=F32)
        a_ab = jnp.where(strict, a_all[:GW, :GW], 0.0)
        a_ak = jnp.where(strict, a_all[:GW, GW:], 0.0)
        m_rb = jnp.where(incl, a_all[GW:, :GW], 0.0)
        m_rk = jnp.where(incl, a_all[GW:, GW:], 0.0)

        pw = a_ab
        tinv = eye + a_ab
        for _ in range(int(np.log2(C)) - 1):
            pw = dot16(pw, pw)
            tinv = tinv + dot16(tinv, pw)

        akv = dot16(a_ak, v_bd)
        x = dot16(tinv, jnp.concatenate([at_bd, akv], axis=1))
        wm_bd, u0_bd = x[:, :GW], x[:, GW:]
        y1_bd = dot16(jnp.concatenate([m_rk, m_rb], axis=1), jnp.concatenate([v_bd, u0_bd], axis=0))
        rm_bd = dot16(m_rb, wm_bd)
        y1 = collapse(y1_bd)
        rm = r_t[:, sl] + collapse(rm_bd)

        qt = dot16(wm_bd.T, bp_bd)
        h1t = dot16(jnp.concatenate([u0_bd, v_bd], axis=0).T, jnp.concatenate([bp_bd, kp_bd], axis=0))

        ht = ht_ref[gi]
        ys.append(y1 + lax.dot_general(rm.astype(BF16), ht.astype(BF16), (((1,), (1,)), ((), ())),
                                       preferred_element_type=F32))
        ht_ref[gi] = ht * p_c[:, sl] + dot16(ht, qt) + h1t

    y = jnp.concatenate(ys, axis=1)
    inv_n = 1.0 / HEAD_DIM
    mean = headsum(y) * inv_n
    dlt = y - mean
    var = headsum(dlt * dlt) * inv_n
    y = dlt * lax.rsqrt(var + RW_GN_EPS) * gn_w + gn_b
    y = y + headsum(r * k2 * r_k) * v
    y_ref[0] = (y * g).astype(BF16)


def _rwkv(rkv, small, vfirst_rkv, lw, la, lg, lv, vecs, ones_bd):
    B, T, _ = rkv.shape
    C = RW_CHUNK
    has_vres = vfirst_rkv is not None
    row3 = lambda b, j: (b, j, 0)
    in_specs = [pl.BlockSpec((1, C, 3 * WIDTH), row3), pl.BlockSpec((1, C, SMALL_W), row3)]
    args = [rkv, small]
    if has_vres:
        in_specs.append(pl.BlockSpec((1, C, WIDTH), lambda b, j: (b, j, 2)))
        args.append(vfirst_rkv)
    w_list = [lw, la, lg] + ([lv] if has_vres else []) + [vecs, ones_bd]
    in_specs += [_const_spec(t.shape) for t in w_list]
    args += w_list
    return pl.pallas_call(
        functools.partial(_rwkv_body, has_vres=has_vres),
        out_shape=jax.ShapeDtypeStruct((B, T, WIDTH), BF16),
        grid=(B, T // C),
        in_specs=in_specs,
        out_specs=pl.BlockSpec((1, C, WIDTH), row3),
        scratch_shapes=[pltpu.VMEM((N_HEADS // RW_GROUP, GW, GW), F32)],
        compiler_params=_cparams(("arbitrary", "arbitrary")),
        name="rwkv",
    )(*args)


def _mla_prep_body(lat_ref, small_ref, pos_ref, gq_ref, gkv_ref, wq_ref, wkv_ref, freq_ref,
                   q_ref, k_ref, v_ref):
    def rms(t, gain):
        return t * lax.rsqrt(jnp.mean(t * t, axis=-1, keepdims=True) + NORM_EPS) * gain

    lat = lat_ref[0]
    qn = rms(lat[:, :MLA_LORA], gq_ref[...]).astype(BF16)
    kvn = rms(lat[:, MLA_LORA:], gkv_ref[...]).astype(BF16)
    q = jnp.dot(qn, wq_ref[...], preferred_element_type=F32)
    kv = jnp.dot(kvn, wkv_ref[...], preferred_element_type=F32)

    ang = pos_ref[0].astype(F32) * freq_ref[...]
    cos, sin = jnp.cos(ang), jnp.sin(ang)
    half = MLA_ROPE // 2
    q1, q2 = q[:, WIDTH:WIDTH + 128], q[:, WIDTH + 128:WIDTH + 256]
    qo1 = q1 * cos - q2 * sin
    qo2 = q1 * sin + q2 * cos
    kr = small_ref[0][:, S_KR:S_KR + MLA_ROPE]
    k1, k2 = kr[:, :half], kr[:, half:]
    c16, s16 = cos[:, :half], sin[:, :half]
    ko = jnp.concatenate([k1 * c16 - k2 * s16, k1 * s16 + k2 * c16], axis=-1)

    scale = MLA_QK ** -0.5
    for hd in range(N_HEADS):
        hs = slice(hd * HEAD_DIM, (hd + 1) * HEAD_DIM)
        ps = slice(hd * half, (hd + 1) * half)
        qh = jnp.concatenate([q[:, hs], qo1[:, ps], qo2[:, ps]], axis=-1) * scale
        q_ref[0, hd] = qh.astype(BF16)
        k_ref[0, hd] = jnp.concatenate([kv[:, hs], ko], axis=-1).astype(BF16)
        v_ref[0, hd] = kv[:, WIDTH + hd * HEAD_DIM:WIDTH + (hd + 1) * HEAD_DIM].astype(BF16)


def _mla_prep(lat, small, positions, gq, gkv, wq, wkv, freq, *, tm=512):
    B, T, _ = lat.shape
    row3 = lambda b, j: (b, j, 0)
    hm = lambda d: pl.BlockSpec((1, N_HEADS, tm, d), lambda b, j: (b, 0, j, 0))
    return pl.pallas_call(
        _mla_prep_body,
        out_shape=(jax.ShapeDtypeStruct((B, N_HEADS, T, MLA_QK), BF16),
                   jax.ShapeDtypeStruct((B, N_HEADS, T, MLA_QK), BF16),
                   jax.ShapeDtypeStruct((B, N_HEADS, T, HEAD_DIM), BF16)),
        grid=(B, T // tm),
        in_specs=[pl.BlockSpec((1, tm, 2 * MLA_LORA), row3),
                  pl.BlockSpec((1, tm, SMALL_W), row3),
                  pl.BlockSpec((1, tm, 1), row3),
                  _const_spec(gq.shape), _const_spec(gkv.shape),
                  _const_spec(wq.shape), _const_spec(wkv.shape), _const_spec(freq.shape)],
        out_specs=(hm(MLA_QK), hm(MLA_QK), hm(HEAD_DIM)),
        compiler_params=_cparams(("arbitrary", "arbitrary")),
        name="mla_prep",
    )(lat, small, positions, gq, gkv, wq, wkv, freq)


def _attn_body(*refs, tq, has_bias):
    if has_bias:
        q_ref, k_ref, v_ref, cq_ref, ck_ref, o_ref, m_ref, l_ref, acc_ref = refs
    else:
        q_ref, k_ref, v_ref, o_ref, m_ref, l_ref, acc_ref = refs
    hp = pl.program_id(1)
    qi = pl.program_id(2)
    tk = tq

    m_ref[...] = jnp.full_like(m_ref, NEG_BIG)
    l_ref[...] = jnp.zeros_like(l_ref)
    acc_ref[...] = jnp.zeros_like(acc_ref)

    def step(ki, diag):
        k0 = pl.multiple_of(ki * tk, tk)
        for i in range(2):
            q = q_ref[0, i]
            kb = k_ref[0, i, pl.ds(k0, tk), :]
            vb = v_ref[0, i, pl.ds(k0, tk), :]
            s = lax.dot_general(q, kb, (((1,), (1,)), ((), ())), preferred_element_type=F32)
            if has_bias:
                hd = 2 * hp + i
                cq = cq_ref[0, i]
                ck = ck_ref[0, hd, pl.ds(ki, 1), :]
                s = s + (cq - ck)
            if diag:
                ri = lax.broadcasted_iota(jnp.int32, (tq, tk), 0)
                ci = lax.broadcasted_iota(jnp.int32, (tq, tk), 1)
                s = jnp.where(ri >= ci, s, NEG_BIG)
            m_old = m_ref[i]
            m_new = jnp.maximum(m_old, jnp.max(s, axis=-1, keepdims=True))
            p = jnp.exp(s - m_new)
            alpha = jnp.exp(m_old - m_new)
            l_ref[i] = alpha * l_ref[i] + jnp.sum(p, axis=-1, keepdims=True)
            acc_ref[i] = alpha * acc_ref[i] + jnp.dot(p.astype(BF16), vb, preferred_element_type=F32)
            m_ref[i] = m_new

    def loop_body(ki, carry):
        step(ki, False)
        return carry

    lax.fori_loop(0, qi, loop_body, 0)
    step(qi, True)

    o_ref[0] = jnp.concatenate([acc_ref[i] / l_ref[i] for i in range(2)], axis=-1).astype(BF16)


def _attn(q, k, v, cum_col=None, cum_row=None, *, tq=512):
    B, H, T, dk = q.shape
    dv = v.shape[-1]
    nq = T // tq
    has_bias = cum_col is not None
    qspec = lambda d: pl.BlockSpec((1, 2, tq, d), lambda b, p, i: (b, p, i, 0))
    kspec = lambda d: pl.BlockSpec((1, 2, T, d), lambda b, p, i: (b, p, 0, 0))
    in_specs = [qspec(dk), kspec(dk), kspec(dv)]
    args = [q, k, v]
    if has_bias:
        in_specs += [pl.BlockSpec((1, 2, tq, 1), lambda b, p, i: (b, p, i, 0)),
                     pl.BlockSpec((1, H, nq, tq), lambda b, p, i: (b, 0, 0, 0))]
        args += [cum_col, cum_row.reshape(B, H, nq, tq)]
    return pl.pallas_call(
        functools.partial(_attn_body, tq=tq, has_bias=has_bias),
        out_shape=jax.ShapeDtypeStruct((B, T, H * dv), BF16),
        grid=(B, H // 2, nq),
        in_specs=in_specs,
        out_specs=pl.BlockSpec((1, tq, 2 * dv), lambda b, p, i: (b, i, p)),
        scratch_shapes=[pltpu.VMEM((2, tq, 1), F32), pltpu.VMEM((2, tq, 1), F32),
                        pltpu.VMEM((2, tq, dv), F32)],
        compiler_params=_cparams(("arbitrary", "arbitrary", "arbitrary")),
        name="attn_fox" if has_bias else "attn_mla",
    )(*args)


def _merge_body(yr_ref, yf_ref, ym_ref, gate_ref, x_ref, mod_ref, wb_ref, wo_ref, o_ref):
    D = D_MODEL
    mixed = None
    for gi, y_ref in enumerate((yr_ref, yf_ref, ym_ref)):
        bd = jnp.dot(y_ref[0], wb_ref[gi], preferred_element_type=F32)
        term = gate_ref[0, :, gi * D:(gi + 1) * D].astype(F32) * bd
        mixed = term if mixed is None else mixed + term
    out = jnp.dot(mixed.astype(BF16), wo_ref[...], preferred_element_type=F32)
    o_ref[0] = x_ref[0] + mod_ref[0, 2:3, :] * out


def _merge(y_rw, y_fox, y_mla, gates, x, mod, wb, wo, *, tm=512):
    B, T, D = x.shape
    row3 = lambda b, j: (b, j, 0)
    yspec = pl.BlockSpec((1, tm, WIDTH), row3)
    return pl.pallas_call(
        _merge_body,
        out_shape=jax.ShapeDtypeStruct((B, T, D), F32),
        grid=(B, T // tm),
        in_specs=[yspec, yspec, yspec,
                  pl.BlockSpec((1, tm, N_BRANCH * D), row3),
                  pl.BlockSpec((1, tm, D), row3),
                  pl.BlockSpec((1, 6, D), lambda b, j: (b, 0, 0)),
                  _const_spec(wb.shape), _const_spec(wo.shape)],
        out_specs=pl.BlockSpec((1, tm, D), row3),
        compiler_params=_cparams(("arbitrary", "arbitrary")),
        name="merge",
    )(y_rw, y_fox, y_mla, gates, x, mod, wb, wo)


def _ffn_body(x_ref, mod_ref, g_ref, wup_ref, cw_ref, cb_ref, wdn_ref, fg_ref, o_ref, carry_ref,
              *, tm, fc, final):
    j = pl.program_id(1)

    @pl.when(j == 0)
    def _():
        carry_ref[...] = jnp.zeros_like(carry_ref)

    x = x_ref[0]
    hb = _modulated_norm(x, g_ref[...], mod_ref[0, 3:4, :], mod_ref[0, 4:5, :]).astype(BF16)
    row = lax.broadcasted_iota(jnp.int32, (tm, 1), 0)

    def conv(c0):
        u = jnp.dot(hb, wup_ref[:, c0:c0 + fc], preferred_element_type=F32)
        t2 = carry_ref[0:1, c0:c0 + fc]
        t1 = carry_ref[1:2, c0:c0 + fc]
        carry_ref[0:2, c0:c0 + fc] = u[tm - 2:tm, :]
        p1 = jnp.where(row == 0, t1, pltpu.roll(u, 1, 0))
        p2 = jnp.where(row == 0, t2, jnp.where(row == 1, t1, pltpu.roll(u, 2, 0)))
        return (u * cw_ref[2:3, c0:c0 + fc] + p1 * cw_ref[1:2, c0:c0 + fc]
                + p2 * cw_ref[0:1, c0:c0 + fc] + cb_ref[:, c0:c0 + fc])

    acc = None
    for c in range(D_FF // fc):
        ug = conv(c * fc)
        uv = conv(D_FF + c * fc)
        act = (ug * _sigmoid(ug) * uv).astype(BF16)
        part = jnp.dot(act, wdn_ref[c * fc:(c + 1) * fc, :], preferred_element_type=F32)
        acc = part if acc is None else acc + part
    out = x + mod_ref[0, 5:6, :] * acc
    if final:
        out = out * lax.rsqrt(jnp.mean(out * out, axis=-1, keepdims=True) + NORM_EPS) * fg_ref[...]
    o_ref[0] = out


def _ffn(x, mod, gain, w_up, conv_w, conv_b, w_down, final_g, *, final, tm=512, fc=256):
    B, T, D = x.shape
    row3 = lambda b, j: (b, j, 0)
    return pl.pallas_call(
        functools.partial(_ffn_body, tm=tm, fc=fc, final=final),
        out_shape=jax.ShapeDtypeStruct((B, T, D), F32),
        grid=(B, T // tm),
        in_specs=[pl.BlockSpec((1, tm, D), row3),
                  pl.BlockSpec((1, 6, D), lambda b, j: (b, 0, 0)),
                  _const_spec(gain.shape), _const_spec(w_up.shape), _const_spec(conv_w.shape),
                  _const_spec(conv_b.shape), _const_spec(w_down.shape), _const_spec(final_g.shape)],
        out_specs=pl.BlockSpec((1, tm, D), row3),
        scratch_shapes=[pltpu.VMEM((8, 2 * D_FF), F32)],
        compiler_params=_cparams(("arbitrary", "arbitrary")),
        name="ffn",
    )(x, mod, gain, w_up, conv_w, conv_b, w_down, final_g)


def _cat_weight(w_in_i, vres_down_i):
    D = w_in_i.shape[0]
    o_wl, o_k, o_v, o_al, o_gl, rw_cols = 512, 576, 1088, 1600, 1664, 1792
    o_fox = rw_cols
    o_ff = o_fox + 3 * WIDTH
    o_mq = o_ff + N_HEADS
    o_kr = o_mq + 2 * MLA_LORA
    o_gate = o_kr + MLA_ROPE
    vres = jnp.zeros((D, RW_V_LORA), w_in_i.dtype) if vres_down_i is None else vres_down_i
    pad = jnp.zeros((D, SMALL_W - (S_FF + N_HEADS)), w_in_i.dtype)
    cols = [w_in_i[:, 0:512], w_in_i[:, o_k:o_k + 512], w_in_i[:, o_v:o_v + 512],
            w_in_i[:, o_wl:o_wl + 64], w_in_i[:, o_al:o_al + 64], w_in_i[:, o_gl:o_gl + 128],
            vres, w_in_i[:, o_kr:o_kr + MLA_ROPE], w_in_i[:, o_ff:o_ff + N_HEADS], pad,
            w_in_i[:, o_fox:o_fox + 3 * WIDTH],
            w_in_i[:, o_mq:o_mq + 2 * MLA_LORA],
            w_in_i[:, o_gate:o_gate + N_BRANCH * D_MODEL]]
    w_ff_t = w_in_i[:, o_ff:o_ff + N_HEADS].T
    return jnp.concatenate(cols, axis=1).astype(BF16), w_ff_t.astype(BF16)


def _cat_mu(mu_i, vres_mu_i):
    o_wl, o_k, o_v, o_al, o_gl = 512, 576, 1088, 1600, 1664
    vres = jnp.zeros((RW_V_LORA,), F32) if vres_mu_i is None else vres_mu_i
    tail = jnp.zeros((SMALL_W - S_KR,), F32)
    return jnp.concatenate([mu_i[0:512], mu_i[o_k:o_k + 512], mu_i[o_v:o_v + 512],
                            mu_i[o_wl:o_wl + 64], mu_i[o_al:o_al + 64], mu_i[o_gl:o_gl + 128],
                            vres, tail]).reshape(1, N_SHIFT)


def _mla_weights(w_q_up, w_kv_up):
    half = MLA_ROPE // 2
    wq = w_q_up.reshape(MLA_LORA, N_HEADS, MLA_QK)
    wq = jnp.concatenate([wq[:, :, :MLA_NOPE].reshape(MLA_LORA, -1),
                          wq[:, :, MLA_NOPE:MLA_NOPE + half].reshape(MLA_LORA, -1),
                          wq[:, :, MLA_NOPE + half:].reshape(MLA_LORA, -1)], axis=1)
    wkv = w_kv_up.reshape(MLA_LORA, N_HEADS, MLA_NOPE + HEAD_DIM)
    wkv = jnp.concatenate([wkv[:, :, :MLA_NOPE].reshape(MLA_LORA, -1),
                           wkv[:, :, MLA_NOPE:].reshape(MLA_LORA, -1)], axis=1)
    return wq.astype(BF16), wkv.astype(BF16)


def kernel(x, c, positions, norm1_g, norm2_g, w_ada, b_ada, w_in, mu_shift, rw_w_up, rw_w0, rw_a_up, rw_a0, rw_g_up, rw_k_k, rw_k_a, rw_r_k, rw_gn_w, rw_gn_b, rw_vres_down, rw_vres_mu, rw_vres_up, rw_v0, fox_b_f, mla_q_norm_g, mla_w_q_up, mla_kv_norm_g, mla_w_kv_up, w_branch, w_o, ffn_w_up, ffn_conv_w, ffn_conv_b, ffn_w_down, final_g):
    B, T, D = x.shape
    depth = w_in.shape[0]
    half = MLA_ROPE // 2
    inv_freq = np.power(np.float32(ROPE_BASE), -np.arange(half, dtype=np.float32) / np.float32(half))
    freq = jnp.asarray(np.tile(inv_freq.astype(np.float32), 128 // half).reshape(1, 128))
    hid = np.arange(WIDTH) // HEAD_DIM
    ones_bd = jnp.asarray((hid[:, None] == hid[None, :]).astype(np.float32), BF16)
    pos3 = positions.reshape(B, T, 1)

    mods = _ada(c, w_ada, b_ada).reshape(depth, B, 6, D)
    rkv0 = None
    for i in range(depth):
        mod = mods[i]
        w_cat, w_ff_t = _cat_weight(w_in[i], None if i == 0 else rw_vres_down[i - 1])
        mu = _cat_mu(mu_shift[i], None if i == 0 else rw_vres_mu[i - 1])
        rkv, small, fq, fk, fv, lat, gates, cum_r = _proj(
            x, mod, norm1_g[i].reshape(1, D), w_cat, w_ff_t, mu, fox_b_f[i].reshape(N_HEADS, 1))

        zero = jnp.zeros((WIDTH,), F32)
        vecs = jnp.stack([rw_w0[i], rw_a0[i], rw_k_k[i], rw_k_a[i], rw_r_k[i], rw_gn_w[i], rw_gn_b[i],
                          zero if i == 0 else rw_v0[i - 1]])
        y_rw = _rwkv(rkv, small, None if i == 0 else rkv0,
                     rw_w_up[i].astype(BF16), rw_a_up[i].astype(BF16), rw_g_up[i].astype(BF16),
                     None if i == 0 else rw_vres_up[i - 1].astype(BF16), vecs, ones_bd)
        if i == 0:
            rkv0 = rkv

        y_fox = _attn(fq, fk, fv, cum_r.reshape(B, N_HEADS, T, 1), cum_r)

        wq, wkv = _mla_weights(mla_w_q_up[i], mla_w_kv_up[i])
        mq, mk, mv = _mla_prep(lat, small, pos3, mla_q_norm_g[i].reshape(1, -1),
                               mla_kv_norm_g[i].reshape(1, -1), wq, wkv, freq)
        y_mla = _attn(mq, mk, mv)

        x = _merge(y_rw, y_fox, y_mla, gates, x, mod,
                   w_branch[i].astype(BF16), w_o[i].astype(BF16))
        x = _ffn(x, mod, norm2_g[i].reshape(1, D), ffn_w_up[i].astype(BF16), ffn_conv_w[i],
                 ffn_conv_b[i].reshape(1, -1), ffn_w_down[i].astype(BF16), final_g.reshape(1, D),
                 final=(i == depth - 1))
    return x
```

```python
import functools

import numpy as np
import jax
import jax.numpy as jnp
from jax import lax
from jax.experimental import pallas as pl
from jax.experimental.pallas import tpu as pltpu

F32 = jnp.float32
BF16 = jnp.bfloat16
HIGHEST = lax.Precision.HIGHEST

D_MODEL = 1024
HEAD_DIM = 64
N_HEADS = 8
WIDTH = N_HEADS * HEAD_DIM
RW_DECAY_LORA = 64
RW_A_LORA = 64
RW_V_LORA = 32
RW_G_LORA = 128
RW_GN_EPS = 64e-5
MLA_NOPE = 64
MLA_ROPE = 32
MLA_QK = MLA_NOPE + MLA_ROPE
MLA_LORA = 256
ROPE_BASE = 10000.0
N_BRANCH = 3
D_FF = 2816
NORM_EPS = 1e-6
NEG_BIG = -1e30

VMEM_LIMIT_BYTES = 56 * 1024 * 1024

C_RKV = 0
C_SMALL = 3 * WIDTH
SMALL_W = 384
S_WL, S_AL, S_GL, S_VR, S_KR, S_FF = 0, 64, 128, 256, 288, 320
C_FOX = C_SMALL + SMALL_W
C_LAT = C_FOX + 2 * WIDTH
C_GATE = C_LAT + 2 * MLA_LORA
N_COLS = C_GATE + N_BRANCH * D_MODEL
N_SHIFT = C_FOX

LOG2E = 1.4426950408889634
FOX_EXTRA = 8
FOX_QK = HEAD_DIM + FOX_EXTRA

RW_CHUNK = 64
RW_GROUP = 4
GW = RW_GROUP * HEAD_DIM


def _cparams(sem):
    return pltpu.CompilerParams(dimension_semantics=sem, vmem_limit_bytes=VMEM_LIMIT_BYTES)


def _const_spec(shape):
    n = len(shape)
    return pl.BlockSpec(shape, lambda *_: (0,) * n, pipeline_mode=pl.Buffered(1))


def _sigmoid(z):
    return 1.0 / (1.0 + jnp.exp(-z))


def _softplus(z):
    return jnp.maximum(z, 0.0) + jnp.log1p(jnp.exp(-jnp.abs(z)))


def _modulated_norm(x, gain, shift, scale):
    ms = jnp.mean(x * x, axis=-1, keepdims=True)
    return (x * lax.rsqrt(ms + NORM_EPS) * gain) * (1.0 + scale) + shift


def _ada_body(c_ref, w_ref, b_ref, o_ref):
    c = c_ref[...]
    ca = c * _sigmoid(c)
    o_ref[0] = jnp.dot(ca, w_ref[0], precision=HIGHEST, preferred_element_type=F32) + b_ref[0]


def _ada(c, w_ada, b_ada):
    L, D, N6 = w_ada.shape
    B = c.shape[0]
    tn = 1536
    return pl.pallas_call(
        _ada_body,
        out_shape=jax.ShapeDtypeStruct((L, B, N6), F32),
        grid=(L, N6 // tn),
        in_specs=[pl.BlockSpec((B, D), lambda l, n: (0, 0)),
                  pl.BlockSpec((1, D, tn), lambda l, n: (l, 0, n)),
                  pl.BlockSpec((1, 1, tn), lambda l, n: (l, 0, n))],
        out_specs=pl.BlockSpec((1, B, tn), lambda l, n: (l, 0, n)),
        compiler_params=_cparams(("arbitrary", "arbitrary")),
        name="ada",
    )(c, w_ada, b_ada.reshape(L, 1, N6))


def _bf16_round(t):
    return t.astype(BF16).astype(F32)


def _proj_body(x_ref, mod_ref, g_ref, w_ref, wvt_ref, mu_ref, bf_ref,
               rkv_ref, small_ref, fq_ref, fk_ref, fvt_ref, lat_ref, gate_ref,
               carry_ref, cc_ref, *, tm):
    j = pl.program_id(1)

    @pl.when(j == 0)
    def _():
        carry_ref[...] = jnp.zeros_like(carry_ref)
        cc_ref[...] = jnp.zeros_like(cc_ref)

    h = _modulated_norm(x_ref[0], g_ref[...], mod_ref[0, 0:1, :], mod_ref[0, 1:2, :])
    hb = h.astype(BF16)
    row = lax.broadcasted_iota(jnp.int32, (tm, 1), 0)

    def mm(c0, c1):
        return jnp.dot(hb, w_ref[:, c0:c1], preferred_element_type=F32)

    def shifted(c0, c1):
        res = mm(c0, c1)
        last = carry_ref[0:1, c0:c1]
        carry_ref[0:1, c0:c1] = res[tm - 1:tm, :]
        prev = jnp.where(row == 0, last, pltpu.roll(res, 1, 0))
        return res + (prev - res) * mu_ref[:, c0:c1]

    for i in range(3):
        c0 = C_RKV + i * WIDTH
        rkv_ref[0, :, c0:c0 + WIDTH] = shifted(c0, c0 + WIDTH).astype(BF16)
    small = shifted(C_SMALL, C_SMALL + SMALL_W)
    small_ref[0] = small

    z = small[:, S_FF:S_FF + N_HEADS] + bf_ref[...]
    logf = jnp.minimum(z, 0.0) - jnp.log1p(jnp.exp(-jnp.abs(z)))
    ri = lax.broadcasted_iota(jnp.int32, (tm, tm), 0)
    ci = lax.broadcasted_iota(jnp.int32, (tm, tm), 1)
    cum = jnp.dot((ci <= ri).astype(F32), logf, precision=HIGHEST,
                  preferred_element_type=F32) + cc_ref[0:1, 0:N_HEADS]
    cc_ref[0:1, 0:N_HEADS] = cum[tm - 1:tm, :]
    cum2 = cum * LOG2E

    fq = mm(C_FOX, C_FOX + WIDTH) * (HEAD_DIM ** -0.5 * LOG2E)
    fk = mm(C_FOX + WIDTH, C_FOX + 2 * WIDTH)
    lane = lax.broadcasted_iota(jnp.int32, (tm, FOX_EXTRA), 1)
    for hd in range(N_HEADS):
        c8 = jnp.broadcast_to(cum2[:, hd:hd + 1], (tm, FOX_EXTRA))
        hi = _bf16_round(c8)
        mid = _bf16_round(c8 - hi)
        lo = _bf16_round(c8 - hi - mid)
        eq = jnp.where(lane == 0, hi, jnp.where(lane == 1, mid, jnp.where(lane == 2, lo,
                       jnp.where(lane < 6, 1.0, 0.0))))
        ek = jnp.where(lane < 3, 1.0, jnp.where(lane == 3, -hi, jnp.where(lane == 4, -mid,
                       jnp.where(lane == 5, -lo, 0.0))))
        hs = slice(hd * HEAD_DIM, (hd + 1) * HEAD_DIM)
        fq_ref[0, hd] = jnp.concatenate([fq[:, hs], eq], axis=-1).astype(BF16)
        fk_ref[0, hd] = jnp.concatenate([fk[:, hs], ek], axis=-1).astype(BF16)

    fvt = lax.dot_general(wvt_ref[...], hb, (((1,), (1,)), ((), ())), preferred_element_type=F32)
    for hd in range(N_HEADS):
        fvt_ref[0, hd] = fvt[hd * HEAD_DIM:(hd + 1) * HEAD_DIM, :].astype(BF16)

    lat_ref[0] = mm(C_LAT, C_GATE)

    for i in range(N_BRANCH * D_MODEL // WIDTH):
        c0 = C_GATE + i * WIDTH
        gate_ref[0, :, i * WIDTH:(i + 1) * WIDTH] = _sigmoid(mm(c0, c0 + WIDTH)).astype(BF16)


def _proj(x, mod, gain, w_cat, w_fv_t, mu, b_f, *, tm=256):
    B, T, D = x.shape
    nt = T // tm
    qk = jax.ShapeDtypeStruct((B, N_HEADS, T, FOX_QK), BF16)
    qk_spec = pl.BlockSpec((1, N_HEADS, tm, FOX_QK), lambda b, j: (b, 0, j, 0))
    row3 = lambda b, j: (b, j, 0)
    return pl.pallas_call(
        functools.partial(_proj_body, tm=tm),
        out_shape=(jax.ShapeDtypeStruct((B, T, 3 * WIDTH), BF16),
                   jax.ShapeDtypeStruct((B, T, SMALL_W), F32),
                   qk, qk,
                   jax.ShapeDtypeStruct((B, N_HEADS, HEAD_DIM, T), BF16),
                   jax.ShapeDtypeStruct((B, T, 2 * MLA_LORA), F32),
                   jax.ShapeDtypeStruct((B, T, N_BRANCH * D), BF16)),
        grid=(B, nt),
        in_specs=[pl.BlockSpec((1, tm, D), row3),
                  pl.BlockSpec((1, 6, D), lambda b, j: (b, 0, 0)),
                  _const_spec((1, D)),
                  _const_spec((D, N_COLS)),
                  _const_spec((WIDTH, D)),
                  _const_spec((1, N_SHIFT)),
                  _const_spec((1, N_HEADS))],
        out_specs=(pl.BlockSpec((1, tm, 3 * WIDTH), row3),
                   pl.BlockSpec((1, tm, SMALL_W), row3),
                   qk_spec, qk_spec,
                   pl.BlockSpec((1, N_HEADS, HEAD_DIM, tm), lambda b, j: (b, 0, 0, j)),
                   pl.BlockSpec((1, tm, 2 * MLA_LORA), row3),
                   pl.BlockSpec((1, tm, N_BRANCH * D), row3)),
        scratch_shapes=[pltpu.VMEM((8, N_SHIFT), F32),
                        pltpu.VMEM((8, 128), F32)],
        compiler_params=_cparams(("arbitrary", "arbitrary")),
        name="proj",
    )(x, mod, gain, w_cat, w_fv_t, mu, b_f)


def _rwkv_body(*refs, has_vres):
    if has_vres:
        (rkv_ref, small_ref, vfirst_ref, lw_ref, la_ref, lg_ref, lv_ref, vec_ref, ones_ref,
         y_ref, ht_ref) = refs
    else:
        rkv_ref, small_ref, lw_ref, la_ref, lg_ref, vec_ref, ones_ref, y_ref, ht_ref = refs
    C = RW_CHUNK
    j = pl.program_id(1)

    @pl.when(j == 0)
    def _():
        ht_ref[...] = jnp.zeros_like(ht_ref)

    def vec(i):
        return vec_ref[i:i + 1, :]

    w0, a0, k_k, k_a, r_k, gn_w, gn_b, v0 = (vec(i) for i in range(8))
    ones_bd = ones_ref[...]

    def headsum(t):
        hi = t.astype(BF16)
        lo = (t - hi.astype(F32)).astype(BF16)
        return (jnp.dot(hi, ones_bd, preferred_element_type=F32)
                + jnp.dot(lo, ones_bd, preferred_element_type=F32))

    def lora(t, w_ref):
        return jnp.dot(t.astype(BF16), w_ref[...], preferred_element_type=F32)

    r = rkv_ref[0, :, 0:WIDTH].astype(F32)
    k = rkv_ref[0, :, WIDTH:2 * WIDTH].astype(F32)
    v = rkv_ref[0, :, 2 * WIDTH:3 * WIDTH].astype(F32)
    sm = small_ref[0]
    wl = sm[:, S_WL:S_WL + RW_DECAY_LORA]
    al = sm[:, S_AL:S_AL + RW_A_LORA]
    gl = sm[:, S_GL:S_GL + RW_G_LORA]

    w = -_softplus(-(w0 + lora(jnp.tanh(wl), lw_ref))) - 0.5
    logd = -jnp.exp(w)
    a = _sigmoid(a0 + lora(al, la_ref))
    g = lora(_sigmoid(gl), lg_ref)
    if has_vres:
        lvv = sm[:, S_VR:S_VR + RW_V_LORA]
        v = v + (vfirst_ref[0].astype(F32) - v) * _sigmoid(v0 + lora(lvv, lv_ref))
    kk = k * k_k
    kk = kk * lax.rsqrt(jnp.maximum(headsum(kk * kk), 1e-24))
    k2 = k * (1.0 + (a - 1.0) * k_a)

    ri = lax.broadcasted_iota(jnp.int32, (C, C), 0)
    ci = lax.broadcasted_iota(jnp.int32, (C, C), 1)
    cl = jnp.dot((ci <= ri).astype(F32), logd, precision=HIGHEST, preferred_element_type=F32)
    cl_last = cl[C - 1:C, :]
    e_neg = jnp.exp(-cl)
    e_end = jnp.exp(cl_last - cl)
    ka = kk * a
    a_t = kk * jnp.exp(cl - logd)
    r_t = r * jnp.exp(cl)
    b_m = -ka * e_neg
    k_m = k2 * e_neg
    b_p = -ka * e_end
    k_p = k2 * e_end
    p_c = jnp.exp(cl_last)

    bi = lax.broadcasted_iota(jnp.int32, (GW, GW), 0)
    bj = lax.broadcasted_iota(jnp.int32, (GW, GW), 1)
    bd_mask = (bi // HEAD_DIM) == (bj // HEAD_DIM)
    strict = bi > bj
    incl = bi >= bj
    eye = (bi == bj).astype(F32)

    def bd(t):
        return jnp.where(bd_mask, jnp.concatenate([t] * RW_GROUP, axis=0), 0.0)

    def collapse(t):
        return sum(t[hh * C:(hh + 1) * C, :] for hh in range(RW_GROUP))

    def dot16(p, q):
        return jnp.dot(p.astype(BF16), q.astype(BF16), preferred_element_type=F32)

    ys = []
    for gi in range(N_HEADS // RW_GROUP):
        sl = slice(gi * GW, (gi + 1) * GW)
        at_bd, rt_bd, bm_bd, km_bd = bd(a_t[:, sl]), bd(r_t[:, sl]), bd(b_m[:, sl]), bd(k_m[:, sl])
        bp_bd, kp_bd, v_bd = bd(b_p[:, sl]), bd(k_p[:, sl]), bd(v[:, sl])

        lhs = jnp.concatenate([at_bd, rt_bd], axis=0).astype(BF16)
        rhs = jnp.concatenate([bm_bd, km_bd], axis=0).astype(BF16)
        a_all = lax.dot_general(lhs, rhs, (((1,), (1,)), ((), ())), preferred_element_type=F32)
        a_ab = jnp.where(strict, a_all[:GW, :GW], 0.0)
        a_ak = jnp.where(strict, a_all[:GW, GW:], 0.0)
        m_rb = jnp.where(incl, a_all[GW:, :GW], 0.0)
        m_rk = jnp.where(incl, a_all[GW:, GW:], 0.0)

        pw = a_ab
        tinv = eye + a_ab
        for _ in range(int(np.log2(C)) - 1):
            pw = dot16(pw, pw)
            tinv = tinv + dot16(tinv, pw)

        akv = dot16(a_ak, v_bd)
        x = dot16(tinv, jnp.concatenate([at_bd, akv], axis=1))
        wm_bd, u0_bd = x[:, :GW], x[:, GW:]
        y1_bd = dot16(jnp.concatenate([m_rk, m_rb], axis=1), jnp.concatenate([v_bd, u0_bd], axis=0))
        rm_bd = dot16(m_rb, wm_bd)
        y1 = collapse(y1_bd)
        rm = r_t[:, sl] + collapse(rm_bd)

        qt = dot16(wm_bd.T, bp_bd)
        h1t = dot16(jnp.concatenate([u0_bd, v_bd], axis=0).T, jnp.concatenate([bp_bd, kp_bd], axis=0))

        ht = ht_ref[gi]
        ys.append(y1 + lax.dot_general(rm.astype(BF16), ht.astype(BF16), (((1,), (1,)), ((), ())),
                                       preferred_element_type=F32))
        ht_ref[gi] = ht * p_c[:, sl] + dot16(ht, qt) + h1t

    y = jnp.concatenate(ys, axis=1)
    inv_n = 1.0 / HEAD_DIM
    mean = headsum(y) * inv_n
    dlt = y - mean
    var = headsum(dlt * dlt) * inv_n
    y = dlt * lax.rsqrt(var + RW_GN_EPS) * gn_w + gn_b
    y = y + headsum(r * k2 * r_k) * v
    y_ref[0] = (y * g).astype(BF16)


def _rwkv(rkv, small, vfirst_rkv, lw, la, lg, lv, vecs, ones_bd):
    B, T, _ = rkv.shape
    C = RW_CHUNK
    has_vres = vfirst_rkv is not None
    row3 = lambda b, j: (b, j, 0)
    in_specs = [pl.BlockSpec((1, C, 3 * WIDTH), row3), pl.BlockSpec((1, C, SMALL_W), row3)]
    args = [rkv, small]
    if has_vres:
        in_specs.append(pl.BlockSpec((1, C, WIDTH), lambda b, j: (b, j, 2)))
        args.append(vfirst_rkv)
    w_list = [lw, la, lg] + ([lv] if has_vres else []) + [vecs, ones_bd]
    in_specs += [_const_spec(t.shape) for t in w_list]
    args += w_list
    return pl.pallas_call(
        functools.partial(_rwkv_body, has_vres=has_vres),
        out_shape=jax.ShapeDtypeStruct((B, T, WIDTH), BF16),
        grid=(B, T // C),
        in_specs=in_specs,
        out_specs=pl.BlockSpec((1, C, WIDTH), row3),
        scratch_shapes=[pltpu.VMEM((N_HEADS // RW_GROUP, GW, GW), F32)],
        compiler_params=_cparams(("arbitrary", "arbitrary")),
        name="rwkv",
    )(*args)


def _mla_prep_body(lat_ref, small_ref, pos_ref, gq_ref, gkv_ref, wq_ref, wk_ref, wvt_ref, freq_ref,
                   q_ref, k_ref, vt_ref):
    def rms(t, gain):
        return t * lax.rsqrt(jnp.mean(t * t, axis=-1, keepdims=True) + NORM_EPS) * gain

    lat = lat_ref[0]
    qn = rms(lat[:, :MLA_LORA], gq_ref[...]).astype(BF16)
    kvn = rms(lat[:, MLA_LORA:], gkv_ref[...]).astype(BF16)
    q = jnp.dot(qn, wq_ref[...], preferred_element_type=F32)
    kn = jnp.dot(kvn, wk_ref[...], preferred_element_type=F32)
    vt = lax.dot_general(wvt_ref[...], kvn, (((1,), (1,)), ((), ())), preferred_element_type=F32)

    ang = pos_ref[0].astype(F32) * freq_ref[...]
    cos, sin = jnp.cos(ang), jnp.sin(ang)
    half = MLA_ROPE // 2
    q1, q2 = q[:, WIDTH:WIDTH + 128], q[:, WIDTH + 128:WIDTH + 256]
    qo1 = q1 * cos - q2 * sin
    qo2 = q1 * sin + q2 * cos
    kr = small_ref[0][:, S_KR:S_KR + MLA_ROPE]
    k1, k2 = kr[:, :half], kr[:, half:]
    c16, s16 = cos[:, :half], sin[:, :half]
    ko = jnp.concatenate([k1 * c16 - k2 * s16, k1 * s16 + k2 * c16], axis=-1)

    scale = MLA_QK ** -0.5 * LOG2E
    for hd in range(N_HEADS):
        hs = slice(hd * HEAD_DIM, (hd + 1) * HEAD_DIM)
        ps = slice(hd * half, (hd + 1) * half)
        qh = jnp.concatenate([q[:, hs], qo1[:, ps], qo2[:, ps]], axis=-1) * scale
        q_ref[0, hd] = qh.astype(BF16)
        k_ref[0, hd] = jnp.concatenate([kn[:, hs], ko], axis=-1).astype(BF16)
        vt_ref[0, hd] = vt[hs, :].astype(BF16)


def _mla_prep(lat, small, positions, gq, gkv, wq, wk, wv_t, freq, *, tm=512):
    B, T, _ = lat.shape
    row3 = lambda b, j: (b, j, 0)
    hm = lambda d: pl.BlockSpec((1, N_HEADS, tm, d), lambda b, j: (b, 0, j, 0))
    return pl.pallas_call(
        _mla_prep_body,
        out_shape=(jax.ShapeDtypeStruct((B, N_HEADS, T, MLA_QK), BF16),
                   jax.ShapeDtypeStruct((B, N_HEADS, T, MLA_QK), BF16),
                   jax.ShapeDtypeStruct((B, N_HEADS, HEAD_DIM, T), BF16)),
        grid=(B, T // tm),
        in_specs=[pl.BlockSpec((1, tm, 2 * MLA_LORA), row3),
                  pl.BlockSpec((1, tm, SMALL_W), row3),
                  pl.BlockSpec((1, tm, 1), row3),
                  _const_spec(gq.shape), _const_spec(gkv.shape),
                  _const_spec(wq.shape), _const_spec(wk.shape), _const_spec(wv_t.shape),
                  _const_spec(freq.shape)],
        out_specs=(hm(MLA_QK), hm(MLA_QK),
                   pl.BlockSpec((1, N_HEADS, HEAD_DIM, tm), lambda b, j: (b, 0, 0, j))),
        compiler_params=_cparams(("arbitrary", "arbitrary")),
        name="mla_prep",
    )(lat, small, positions, gq, gkv, wq, wk, wv_t, freq)


def _attn_body(q_ref, k_ref, vt_ref, o_ref, m_ref, l_ref, acc_ref, *, tq):
    qi = pl.program_id(2)
    tk = tq

    m_ref[...] = jnp.full_like(m_ref, NEG_BIG)
    l_ref[...] = jnp.zeros_like(l_ref)
    acc_ref[...] = jnp.zeros_like(acc_ref)

    def step(ki, diag):
        k0 = pl.multiple_of(ki * tk, tk)
        for i in range(2):
            q = q_ref[0, i]
            kb = k_ref[0, i, pl.ds(k0, tk), :]
            vtb = vt_ref[0, i, :, pl.ds(k0, tk)]
            st = lax.dot_general(kb, q, (((1,), (1,)), ((), ())), preferred_element_type=F32)
            if diag:
                key = lax.broadcasted_iota(jnp.int32, (tk, tq), 0)
                qry = lax.broadcasted_iota(jnp.int32, (tk, tq), 1)
                st = jnp.where(key <= qry, st, NEG_BIG)
            m_old = m_ref[i]
            m_new = jnp.maximum(m_old, jnp.max(st, axis=0, keepdims=True))
            p = jnp.exp2(st - m_new)
            alpha = jnp.exp2(m_old - m_new)
            l_ref[i] = alpha * l_ref[i] + jnp.sum(p, axis=0, keepdims=True)
            acc_ref[i] = alpha * acc_ref[i] + jnp.dot(vtb, p.astype(BF16), preferred_element_type=F32)
            m_ref[i] = m_new

    def loop_body(ki, carry):
        step(ki, False)
        return carry

    lax.fori_loop(0, qi, loop_body, 0)
    step(qi, True)

    out_t = jnp.concatenate([acc_ref[i] / l_ref[i] for i in range(2)], axis=0)
    o_ref[0] = out_t.T.astype(BF16)


def _attn(q, k, vt, *, name, tq=512):
    B, H, T, dk = q.shape
    dv = vt.shape[2]
    nq = T // tq
    return pl.pallas_call(
        functools.partial(_attn_body, tq=tq),
        out_shape=jax.ShapeDtypeStruct((B, T, H * dv), BF16),
        grid=(B, H // 2, nq),
        in_specs=[pl.BlockSpec((1, 2, tq, dk), lambda b, p, i: (b, p, i, 0)),
                  pl.BlockSpec((1, 2, T, dk), lambda b, p, i: (b, p, 0, 0)),
                  pl.BlockSpec((1, 2, dv, T), lambda b, p, i: (b, p, 0, 0))],
        out_specs=pl.BlockSpec((1, tq, 2 * dv), lambda b, p, i: (b, i, p)),
        scratch_shapes=[pltpu.VMEM((2, 1, tq), F32), pltpu.VMEM((2, 1, tq), F32),
                        pltpu.VMEM((2, dv, tq), F32)],
        compiler_params=_cparams(("arbitrary", "arbitrary", "arbitrary")),
        name=name,
    )(q, k, vt)


def _merge_body(yr_ref, yf_ref, ym_ref, gate_ref, x_ref, mod_ref, wb_ref, wo_ref, o_ref):
    D = D_MODEL
    mixed = None
    for gi, y_ref in enumerate((yr_ref, yf_ref, ym_ref)):
        bd = jnp.dot(y_ref[0], wb_ref[gi], preferred_element_type=F32)
        term = gate_ref[0, :, gi * D:(gi + 1) * D].astype(F32) * bd
        mixed = term if mixed is None else mixed + term
    out = jnp.dot(mixed.astype(BF16), wo_ref[...], preferred_element_type=F32)
    o_ref[0] = x_ref[0] + mod_ref[0, 2:3, :] * out


def _merge(y_rw, y_fox, y_mla, gates, x, mod, wb, wo, *, tm=512):
    B, T, D = x.shape
    row3 = lambda b, j: (b, j, 0)
    yspec = pl.BlockSpec((1, tm, WIDTH), row3)
    return pl.pallas_call(
        _merge_body,
        out_shape=jax.ShapeDtypeStruct((B, T, D), F32),
        grid=(B, T // tm),
        in_specs=[yspec, yspec, yspec,
                  pl.BlockSpec((1, tm, N_BRANCH * D), row3),
                  pl.BlockSpec((1, tm, D), row3),
                  pl.BlockSpec((1, 6, D), lambda b, j: (b, 0, 0)),
                  _const_spec(wb.shape), _const_spec(wo.shape)],
        out_specs=pl.BlockSpec((1, tm, D), row3),
        compiler_params=_cparams(("arbitrary", "arbitrary")),
        name="merge",
    )(y_rw, y_fox, y_mla, gates, x, mod, wb, wo)


def _ffn_body(x_ref, mod_ref, g_ref, wup_ref, cw_ref, cb_ref, wdn_ref, fg_ref, o_ref, carry_ref,
              *, tm, fc, final):
    j = pl.program_id(1)

    @pl.when(j == 0)
    def _():
        carry_ref[...] = jnp.zeros_like(carry_ref)

    x = x_ref[0]
    hb = _modulated_norm(x, g_ref[...], mod_ref[0, 3:4, :], mod_ref[0, 4:5, :]).astype(BF16)
    row = lax.broadcasted_iota(jnp.int32, (tm, 1), 0)

    def conv(c0):
        u = jnp.dot(hb, wup_ref[:, c0:c0 + fc], preferred_element_type=F32)
        t2 = carry_ref[0:1, c0:c0 + fc]
        t1 = carry_ref[1:2, c0:c0 + fc]
        carry_ref[0:2, c0:c0 + fc] = u[tm - 2:tm, :]
        p1 = jnp.where(row == 0, t1, pltpu.roll(u, 1, 0))
        p2 = jnp.where(row == 0, t2, jnp.where(row == 1, t1, pltpu.roll(u, 2, 0)))
        return (u * cw_ref[2:3, c0:c0 + fc] + p1 * cw_ref[1:2, c0:c0 + fc]
                + p2 * cw_ref[0:1, c0:c0 + fc] + cb_ref[:, c0:c0 + fc])

    acc = None
    for c in range(D_FF // fc):
        ug = conv(c * fc)
        uv = conv(D_FF + c * fc)
        act = (ug * _sigmoid(ug) * uv).astype(BF16)
        part = jnp.dot(act, wdn_ref[c * fc:(c + 1) * fc, :], preferred_element_type=F32)
        acc = part if acc is None else acc + part
    out = x + mod_ref[0, 5:6, :] * acc
    if final:
        out = out * lax.rsqrt(jnp.mean(out * out, axis=-1, keepdims=True) + NORM_EPS) * fg_ref[...]
    o_ref[0] = out


def _ffn(x, mod, gain, w_up, conv_w, conv_b, w_down, final_g, *, final, tm=512, fc=256):
    B, T, D = x.shape
    row3 = lambda b, j: (b, j, 0)
    return pl.pallas_call(
        functools.partial(_ffn_body, tm=tm, fc=fc, final=final),
        out_shape=jax.ShapeDtypeStruct((B, T, D), F32),
        grid=(B, T // tm),
        in_specs=[pl.BlockSpec((1, tm, D), row3),
                  pl.BlockSpec((1, 6, D), lambda b, j: (b, 0, 0)),
                  _const_spec(gain.shape), _const_spec(w_up.shape), _const_spec(conv_w.shape),
                  _const_spec(conv_b.shape), _const_spec(w_down.shape), _const_spec(final_g.shape)],
        out_specs=pl.BlockSpec((1, tm, D), row3),
        scratch_shapes=[pltpu.VMEM((8, 2 * D_FF), F32)],
        compiler_params=_cparams(("arbitrary", "arbitrary")),
        name="ffn",
    )(x, mod, gain, w_up, conv_w, conv_b, w_down, final_g)


def _cat_weight(w_in_i, vres_down_i):
    D = w_in_i.shape[0]
    o_wl, o_k, o_v, o_al, o_gl, rw_cols = 512, 576, 1088, 1600, 1664, 1792
    o_fox = rw_cols
    o_ff = o_fox + 3 * WIDTH
    o_mq = o_ff + N_HEADS
    o_kr = o_mq + 2 * MLA_LORA
    o_gate = o_kr + MLA_ROPE
    vres = jnp.zeros((D, RW_V_LORA), w_in_i.dtype) if vres_down_i is None else vres_down_i
    pad = jnp.zeros((D, SMALL_W - (S_FF + N_HEADS)), w_in_i.dtype)
    cols = [w_in_i[:, 0:512], w_in_i[:, o_k:o_k + 512], w_in_i[:, o_v:o_v + 512],
            w_in_i[:, o_wl:o_wl + 64], w_in_i[:, o_al:o_al + 64], w_in_i[:, o_gl:o_gl + 128],
            vres, w_in_i[:, o_kr:o_kr + MLA_ROPE], w_in_i[:, o_ff:o_ff + N_HEADS], pad,
            w_in_i[:, o_fox:o_fox + 2 * WIDTH],
            w_in_i[:, o_mq:o_mq + 2 * MLA_LORA],
            w_in_i[:, o_gate:o_gate + N_BRANCH * D_MODEL]]
    w_fv_t = w_in_i[:, o_fox + 2 * WIDTH:o_fox + 3 * WIDTH].T
    return jnp.concatenate(cols, axis=1).astype(BF16), w_fv_t.astype(BF16)


def _cat_mu(mu_i, vres_mu_i):
    o_wl, o_k, o_v, o_al, o_gl = 512, 576, 1088, 1600, 1664
    vres = jnp.zeros((RW_V_LORA,), F32) if vres_mu_i is None else vres_mu_i
    tail = jnp.zeros((SMALL_W - S_KR,), F32)
    return jnp.concatenate([mu_i[0:512], mu_i[o_k:o_k + 512], mu_i[o_v:o_v + 512],
                            mu_i[o_wl:o_wl + 64], mu_i[o_al:o_al + 64], mu_i[o_gl:o_gl + 128],
                            vres, tail]).reshape(1, N_SHIFT)


def _mla_weights(w_q_up, w_kv_up):
    half = MLA_ROPE // 2
    wq = w_q_up.reshape(MLA_LORA, N_HEADS, MLA_QK)
    wq = jnp.concatenate([wq[:, :, :MLA_NOPE].reshape(MLA_LORA, -1),
                          wq[:, :, MLA_NOPE:MLA_NOPE + half].reshape(MLA_LORA, -1),
                          wq[:, :, MLA_NOPE + half:].reshape(MLA_LORA, -1)], axis=1)
    wkv = w_kv_up.reshape(MLA_LORA, N_HEADS, MLA_NOPE + HEAD_DIM)
    wk = wkv[:, :, :MLA_NOPE].reshape(MLA_LORA, -1)
    wv_t = wkv[:, :, MLA_NOPE:].reshape(MLA_LORA, -1).T
    return wq.astype(BF16), wk.astype(BF16), wv_t.astype(BF16)


def kernel(x, c, positions, norm1_g, norm2_g, w_ada, b_ada, w_in, mu_shift, rw_w_up, rw_w0, rw_a_up, rw_a0, rw_g_up, rw_k_k, rw_k_a, rw_r_k, rw_gn_w, rw_gn_b, rw_vres_down, rw_vres_mu, rw_vres_up, rw_v0, fox_b_f, mla_q_norm_g, mla_w_q_up, mla_kv_norm_g, mla_w_kv_up, w_branch, w_o, ffn_w_up, ffn_conv_w, ffn_conv_b, ffn_w_down, final_g):
    B, T, D = x.shape
    depth = w_in.shape[0]
    half = MLA_ROPE // 2
    inv_freq = np.power(np.float32(ROPE_BASE), -np.arange(half, dtype=np.float32) / np.float32(half))
    freq = jnp.asarray(np.tile(inv_freq.astype(np.float32), 128 // half).reshape(1, 128))
    hid = np.arange(WIDTH) // HEAD_DIM
    ones_bd = jnp.asarray((hid[:, None] == hid[None, :]).astype(np.float32), BF16)
    pos3 = positions.reshape(B, T, 1)

    mods = _ada(c, w_ada, b_ada).reshape(depth, B, 6, D)
    rkv0 = None
    for i in range(depth):
        mod = mods[i]
        w_cat, w_fv_t = _cat_weight(w_in[i], None if i == 0 else rw_vres_down[i - 1])
        mu = _cat_mu(mu_shift[i], None if i == 0 else rw_vres_mu[i - 1])
        rkv, small, fq, fk, fvt, lat, gates = _proj(
            x, mod, norm1_g[i].reshape(1, D), w_cat, w_fv_t, mu, fox_b_f[i].reshape(1, N_HEADS))

        zero = jnp.zeros((WIDTH,), F32)
        vecs = jnp.stack([rw_w0[i], rw_a0[i], rw_k_k[i], rw_k_a[i], rw_r_k[i], rw_gn_w[i], rw_gn_b[i],
                          zero if i == 0 else rw_v0[i - 1]])
        y_rw = _rwkv(rkv, small, None if i == 0 else rkv0,
                     rw_w_up[i].astype(BF16), rw_a_up[i].astype(BF16), rw_g_up[i].astype(BF16),
                     None if i == 0 else rw_vres_up[i - 1].astype(BF16), vecs, ones_bd)
        if i == 0:
            rkv0 = rkv

        y_fox = _attn(fq, fk, fvt, name="attn_fox")

        wq, wk, wv_t = _mla_weights(mla_w_q_up[i], mla_w_kv_up[i])
        mq, mk, mvt = _mla_prep(lat, small, pos3, mla_q_norm_g[i].reshape(1, -1),
                                mla_kv_norm_g[i].reshape(1, -1), wq, wk, wv_t, freq)
        y_mla = _attn(mq, mk, mvt, name="attn_mla")

        x = _merge(y_rw, y_fox, y_mla, gates, x, mod,
                   w_branch[i].astype(BF16), w_o[i].astype(BF16))
        x = _ffn(x, mod, norm2_g[i].reshape(1, D), ffn_w_up[i].astype(BF16), ffn_conv_w[i],
                 ffn_conv_b[i].reshape(1, -1), ffn_w_down[i].astype(BF16), final_g.reshape(1, D),
                 final=(i == depth - 1))
    return x
```

```python
import functools

import numpy as np
import jax
import jax.numpy as jnp
from jax import lax
from jax.experimental import pallas as pl
from jax.experimental.pallas import tpu as pltpu

F32 = jnp.float32
BF16 = jnp.bfloat16
HIGHEST = lax.Precision.HIGHEST

D_MODEL = 1024
HEAD_DIM = 64
N_HEADS = 8
WIDTH = N_HEADS * HEAD_DIM
RW_DECAY_LORA = 64
RW_A_LORA = 64
RW_V_LORA = 32
RW_G_LORA = 128
RW_GN_EPS = 64e-5
MLA_NOPE = 64
MLA_ROPE = 32
MLA_QK = MLA_NOPE + MLA_ROPE
MLA_LORA = 256
ROPE_BASE = 10000.0
N_BRANCH = 3
D_FF = 2816
NORM_EPS = 1e-6
NEG_BIG = -1e30

VMEM_LIMIT_BYTES = 56 * 1024 * 1024

C_RKV = 0
C_SMALL = 3 * WIDTH
SMALL_W = 384
S_WL, S_AL, S_GL, S_VR, S_KR, S_FF = 0, 64, 128, 256, 288, 320
C_FOX = C_SMALL + SMALL_W
C_LAT = C_FOX + 2 * WIDTH
C_GATE = C_LAT + 2 * MLA_LORA
N_COLS = C_GATE + N_BRANCH * D_MODEL
N_SHIFT = C_FOX

LOG2E = 1.4426950408889634
FOX_EXTRA = 8
FOX_QK = HEAD_DIM + FOX_EXTRA
VT_ROWS = HEAD_DIM + 16


def _ones_rows(n):
    r = lax.broadcasted_iota(jnp.int32, (VT_ROWS - HEAD_DIM, n), 0)
    return jnp.where(r == 0, 1.0, 0.0).astype(BF16)

RW_CHUNK = 64
RW_GROUP = 2
GW = RW_GROUP * HEAD_DIM


def _cparams(sem):
    return pltpu.CompilerParams(dimension_semantics=sem, vmem_limit_bytes=VMEM_LIMIT_BYTES)


def _const_spec(shape):
    n = len(shape)
    return pl.BlockSpec(shape, lambda *_: (0,) * n, pipeline_mode=pl.Buffered(1))


def _sigmoid(z):
    return 1.0 / (1.0 + jnp.exp(-z))


def _softplus(z):
    return jnp.maximum(z, 0.0) + jnp.log1p(jnp.exp(-jnp.abs(z)))


def _modulated_norm(x, gain, shift, scale):
    ms = jnp.mean(x * x, axis=-1, keepdims=True)
    return (x * lax.rsqrt(ms + NORM_EPS) * gain) * (1.0 + scale) + shift


def _ada_body(c_ref, w_ref, b_ref, o_ref):
    c = c_ref[...]
    ca = c * _sigmoid(c)
    o_ref[0] = jnp.dot(ca, w_ref[0], precision=HIGHEST, preferred_element_type=F32) + b_ref[0]


def _ada(c, w_ada, b_ada):
    L, D, N6 = w_ada.shape
    B = c.shape[0]
    tn = 1536
    return pl.pallas_call(
        _ada_body,
        out_shape=jax.ShapeDtypeStruct((L, B, N6), F32),
        grid=(L, N6 // tn),
        in_specs=[pl.BlockSpec((B, D), lambda l, n: (0, 0)),
                  pl.BlockSpec((1, D, tn), lambda l, n: (l, 0, n)),
                  pl.BlockSpec((1, 1, tn), lambda l, n: (l, 0, n))],
        out_specs=pl.BlockSpec((1, B, tn), lambda l, n: (l, 0, n)),
        compiler_params=_cparams(("arbitrary", "arbitrary")),
        name="ada",
    )(c, w_ada, b_ada.reshape(L, 1, N6))


def _bf16_round(t):
    return t.astype(BF16).astype(F32)


def _proj_body(x_ref, mod_ref, g_ref, w_ref, wvt_ref, mu_ref, bf_ref,
               rkv_ref, small_ref, fq_ref, fk_ref, fvt_ref, lat_ref, gate_ref,
               carry_ref, cc_ref, *, tm):
    j = pl.program_id(1)

    @pl.when(j == 0)
    def _():
        carry_ref[...] = jnp.zeros_like(carry_ref)
        cc_ref[...] = jnp.zeros_like(cc_ref)

    h = _modulated_norm(x_ref[0], g_ref[...], mod_ref[0, 0:1, :], mod_ref[0, 1:2, :])
    hb = h.astype(BF16)
    row = lax.broadcasted_iota(jnp.int32, (tm, 1), 0)

    def mm(c0, c1):
        return jnp.dot(hb, w_ref[:, c0:c1], preferred_element_type=F32)

    def shifted(c0, c1):
        res = mm(c0, c1)
        last = carry_ref[0:1, c0:c1]
        carry_ref[0:1, c0:c1] = res[tm - 1:tm, :]
        prev = jnp.where(row == 0, last, pltpu.roll(res, 1, 0))
        return res + (prev - res) * mu_ref[:, c0:c1]

    for i in range(3):
        c0 = C_RKV + i * WIDTH
        rkv_ref[0, :, c0:c0 + WIDTH] = shifted(c0, c0 + WIDTH).astype(BF16)
    small = shifted(C_SMALL, C_SMALL + SMALL_W)
    small_ref[0] = small

    z = small[:, S_FF:S_FF + N_HEADS] + bf_ref[...]
    logf = jnp.minimum(z, 0.0) - jnp.log1p(jnp.exp(-jnp.abs(z)))
    ri = lax.broadcasted_iota(jnp.int32, (tm, tm), 0)
    ci = lax.broadcasted_iota(jnp.int32, (tm, tm), 1)
    cum = jnp.dot((ci <= ri).astype(F32), logf, precision=HIGHEST,
                  preferred_element_type=F32) + cc_ref[0:1, 0:N_HEADS]
    cc_ref[0:1, 0:N_HEADS] = cum[tm - 1:tm, :]
    cum2 = cum * LOG2E

    fq = mm(C_FOX, C_FOX + WIDTH) * (HEAD_DIM ** -0.5 * LOG2E)
    fk = mm(C_FOX + WIDTH, C_FOX + 2 * WIDTH)
    lane = lax.broadcasted_iota(jnp.int32, (tm, FOX_EXTRA), 1)
    for hd in range(N_HEADS):
        c8 = jnp.broadcast_to(cum2[:, hd:hd + 1], (tm, FOX_EXTRA))
        hi = _bf16_round(c8)
        mid = _bf16_round(c8 - hi)
        lo = _bf16_round(c8 - hi - mid)
        eq = jnp.where(lane == 0, hi, jnp.where(lane == 1, mid, jnp.where(lane == 2, lo,
                       jnp.where(lane < 6, 1.0, 0.0))))
        ek = jnp.where(lane < 3, 1.0, jnp.where(lane == 3, -hi, jnp.where(lane == 4, -mid,
                       jnp.where(lane == 5, -lo, 0.0))))
        hs = slice(hd * HEAD_DIM, (hd + 1) * HEAD_DIM)
        fq_ref[0, hd] = jnp.concatenate([fq[:, hs], eq], axis=-1).astype(BF16)
        fk_ref[0, hd] = jnp.concatenate([fk[:, hs], ek], axis=-1).astype(BF16)

    fvt = lax.dot_general(wvt_ref[...], hb, (((1,), (1,)), ((), ())), preferred_element_type=F32)
    for hd in range(N_HEADS):
        fvt_ref[0, hd, 0:HEAD_DIM, :] = fvt[hd * HEAD_DIM:(hd + 1) * HEAD_DIM, :].astype(BF16)
        fvt_ref[0, hd, HEAD_DIM:VT_ROWS, :] = _ones_rows(tm)

    lat_ref[0] = mm(C_LAT, C_GATE)

    for i in range(N_BRANCH * D_MODEL // WIDTH):
        c0 = C_GATE + i * WIDTH
        gate_ref[0, :, i * WIDTH:(i + 1) * WIDTH] = _sigmoid(mm(c0, c0 + WIDTH)).astype(BF16)


def _proj(x, mod, gain, w_cat, w_fv_t, mu, b_f, *, tm=256):
    B, T, D = x.shape
    nt = T // tm
    qk = jax.ShapeDtypeStruct((B, N_HEADS, T, FOX_QK), BF16)
    qk_spec = pl.BlockSpec((1, N_HEADS, tm, FOX_QK), lambda b, j: (b, 0, j, 0))
    row3 = lambda b, j: (b, j, 0)
    return pl.pallas_call(
        functools.partial(_proj_body, tm=tm),
        out_shape=(jax.ShapeDtypeStruct((B, T, 3 * WIDTH), BF16),
                   jax.ShapeDtypeStruct((B, T, SMALL_W), F32),
                   qk, qk,
                   jax.ShapeDtypeStruct((B, N_HEADS, VT_ROWS, T), BF16),
                   jax.ShapeDtypeStruct((B, T, 2 * MLA_LORA), F32),
                   jax.ShapeDtypeStruct((B, T, N_BRANCH * D), BF16)),
        grid=(B, nt),
        in_specs=[pl.BlockSpec((1, tm, D), row3),
                  pl.BlockSpec((1, 6, D), lambda b, j: (b, 0, 0)),
                  _const_spec((1, D)),
                  _const_spec((D, N_COLS)),
                  _const_spec((WIDTH, D)),
                  _const_spec((1, N_SHIFT)),
                  _const_spec((1, N_HEADS))],
        out_specs=(pl.BlockSpec((1, tm, 3 * WIDTH), row3),
                   pl.BlockSpec((1, tm, SMALL_W), row3),
                   qk_spec, qk_spec,
                   pl.BlockSpec((1, N_HEADS, VT_ROWS, tm), lambda b, j: (b, 0, 0, j)),
                   pl.BlockSpec((1, tm, 2 * MLA_LORA), row3),
                   pl.BlockSpec((1, tm, N_BRANCH * D), row3)),
        scratch_shapes=[pltpu.VMEM((8, N_SHIFT), F32),
                        pltpu.VMEM((8, 128), F32)],
        compiler_params=_cparams(("arbitrary", "arbitrary")),
        name="proj",
    )(x, mod, gain, w_cat, w_fv_t, mu, b_f)


def _rwkv_body(*refs, has_vres, n_chunks):
    if has_vres:
        (rkv_ref, small_ref, vfirst_ref, lw_ref, la_ref, lg_ref, lv_ref, vec_ref, ones_ref,
         y_ref, ht_ref) = refs
    else:
        rkv_ref, small_ref, lw_ref, la_ref, lg_ref, vec_ref, ones_ref, y_ref, ht_ref = refs
    C = RW_CHUNK
    TC = n_chunks * C
    j = pl.program_id(1)

    @pl.when(j == 0)
    def _():
        ht_ref[...] = jnp.zeros_like(ht_ref)

    def vec(i):
        return vec_ref[i:i + 1, :]

    w0, a0, k_k, k_a, r_k, gn_w, gn_b, v0 = (vec(i) for i in range(8))
    ones_bd = ones_ref[...]
    HW = ones_bd.shape[0]

    def headsum(t):
        tb = t.astype(BF16)
        return jnp.concatenate([jnp.dot(tb[:, c0:c0 + HW], ones_bd, preferred_element_type=F32)
                                for c0 in range(0, WIDTH, HW)], axis=1)

    def lora(t, w_ref):
        return jnp.dot(t.astype(BF16), w_ref[...], preferred_element_type=F32)

    r = rkv_ref[0, :, 0:WIDTH].astype(F32)
    k = rkv_ref[0, :, WIDTH:2 * WIDTH].astype(F32)
    v = rkv_ref[0, :, 2 * WIDTH:3 * WIDTH].astype(F32)
    sm = small_ref[0]
    wl = sm[:, S_WL:S_WL + RW_DECAY_LORA]
    al = sm[:, S_AL:S_AL + RW_A_LORA]
    gl = sm[:, S_GL:S_GL + RW_G_LORA]

    w = -_softplus(-(w0 + lora(jnp.tanh(wl), lw_ref))) - 0.5
    logd = -jnp.exp(w)
    a = _sigmoid(a0 + lora(al, la_ref))
    g = lora(_sigmoid(gl), lg_ref)
    if has_vres:
        lvv = sm[:, S_VR:S_VR + RW_V_LORA]
        v = v + (vfirst_ref[0].astype(F32) - v) * _sigmoid(v0 + lora(lvv, lv_ref))
    kk = k * k_k
    kk = kk * lax.rsqrt(jnp.maximum(headsum(kk * kk), 1e-24))
    k2 = k * (1.0 + (a - 1.0) * k_a)

    ri = lax.broadcasted_iota(jnp.int32, (TC, TC), 0)
    ci = lax.broadcasted_iota(jnp.int32, (TC, TC), 1)
    tri = jnp.where((ci <= ri) & ((ri // C) == (ci // C)), 1.0, 0.0).astype(BF16)
    ld_hi = logd.astype(BF16)
    ld_mid = (logd - ld_hi.astype(F32)).astype(BF16)
    ld_lo = (logd - ld_hi.astype(F32) - ld_mid.astype(F32)).astype(BF16)
    cl = (jnp.dot(tri, ld_hi, preferred_element_type=F32) + jnp.dot(tri, ld_mid, preferred_element_type=F32)
          + jnp.dot(tri, ld_lo, preferred_element_type=F32))
    cl_last = jnp.concatenate([jnp.broadcast_to(cl[(c + 1) * C - 1:(c + 1) * C, :], (C, WIDTH))
                               for c in range(n_chunks)], axis=0)
    e_neg = jnp.exp(-cl)
    e_end = jnp.exp(cl_last - cl)
    ka = kk * a
    a_t = kk * jnp.exp(cl - logd)
    r_t = r * jnp.exp(cl)
    b_m = -ka * e_neg
    k_m = k2 * e_neg
    b_p = -ka * e_end
    k_p = k2 * e_end
    p_c = jnp.exp(cl_last)

    bi = lax.broadcasted_iota(jnp.int32, (GW, GW), 0)
    bj = lax.broadcasted_iota(jnp.int32, (GW, GW), 1)
    bd_mask = (bi // HEAD_DIM) == (bj // HEAD_DIM)
    strict = bi > bj
    incl = bi >= bj
    eye = (bi == bj).astype(F32)

    def bd(t):
        return jnp.where(bd_mask, jnp.concatenate([t] * RW_GROUP, axis=0), 0.0)

    def collapse(t):
        return sum(t[hh * C:(hh + 1) * C, :] for hh in range(RW_GROUP))

    def dot16(p, q):
        return jnp.dot(p.astype(BF16), q.astype(BF16), preferred_element_type=F32)

    chains = [(ch, gi) for ch in range(n_chunks) for gi in range(N_HEADS // RW_GROUP)]
    nc = len(chains)

    def piece(t, c):
        ch, gi = chains[c]
        return t[ch * C:(ch + 1) * C, gi * GW:(gi + 1) * GW]

    def bds(t):
        return [bd(piece(t, c)).astype(BF16) for c in range(nc)]

    at_bd, rt_bd, bm_bd, km_bd, bp_bd, kp_bd, v_bd = (bds(t) for t in (a_t, r_t, b_m, k_m, b_p, k_p, v))
    a_all = [lax.dot_general(jnp.concatenate([at_bd[c], rt_bd[c]], axis=0),
                             jnp.concatenate([bm_bd[c], km_bd[c]], axis=0),
                             (((1,), (1,)), ((), ())), preferred_element_type=F32) for c in range(nc)]
    a_ab = [jnp.where(strict, a_all[c][:GW, :GW], 0.0) for c in range(nc)]
    a_ak = [jnp.where(strict, a_all[c][:GW, GW:], 0.0).astype(BF16) for c in range(nc)]
    m_rb = [jnp.where(incl, a_all[c][GW:, :GW], 0.0).astype(BF16) for c in range(nc)]
    m_rk = [jnp.where(incl, a_all[c][GW:, GW:], 0.0).astype(BF16) for c in range(nc)]

    pw = [t.astype(BF16) for t in a_ab]
    tinv = [eye + t for t in a_ab]
    n_sq = int(np.log2(C)) - 1
    for it in range(n_sq):
        pw = [jnp.dot(pw[c], pw[c], preferred_element_type=F32).astype(BF16) for c in range(nc)]
        tinv = [tinv[c] + dot16(tinv[c], pw[c]) for c in range(nc)]
    tinv = [t.astype(BF16) for t in tinv]

    akv = [dot16(a_ak[c], v_bd[c]) for c in range(nc)]
    x = [dot16(tinv[c], jnp.concatenate([at_bd[c], akv[c].astype(BF16)], axis=1)) for c in range(nc)]
    wm_bd = [t[:, :GW].astype(BF16) for t in x]
    u0_bd = [t[:, GW:].astype(BF16) for t in x]
    y1_bd = [dot16(jnp.concatenate([m_rk[c], m_rb[c]], axis=1),
                   jnp.concatenate([v_bd[c], u0_bd[c]], axis=0)) for c in range(nc)]
    rm_bd = [dot16(m_rb[c], wm_bd[c]) for c in range(nc)]
    qt = [dot16(x[c][:, :GW].T, bp_bd[c]) for c in range(nc)]
    h1t = [dot16(jnp.concatenate([x[c][:, GW:], v_bd[c].astype(F32)], axis=0).T,
                 jnp.concatenate([bp_bd[c], kp_bd[c]], axis=0)) for c in range(nc)]
    local = {}
    for c, (ch, gi) in enumerate(chains):
        local[ch, gi] = (collapse(y1_bd[c]), (piece(r_t, c) + collapse(rm_bd[c])).astype(BF16),
                         qt[c].astype(BF16), h1t[c], p_c[ch * C:ch * C + 1, gi * GW:(gi + 1) * GW])

    y_rows = []
    for ch in range(n_chunks):
        ys = []
        for gi in range(N_HEADS // RW_GROUP):
            y1, rm, qt, h1t, pc = local[ch, gi]
            ht = ht_ref[gi]
            htb = ht.astype(BF16)
            ys.append(y1 + lax.dot_general(rm, htb, (((1,), (1,)), ((), ())), preferred_element_type=F32))
            ht_ref[gi] = ht * pc + jnp.dot(htb, qt, preferred_element_type=F32) + h1t
        y_rows.append(jnp.concatenate(ys, axis=1))
    y = jnp.concatenate(y_rows, axis=0)
    inv_n = 1.0 / HEAD_DIM
    mean = headsum(y) * inv_n
    dlt = y - mean
    var = headsum(dlt * dlt) * inv_n
    y = dlt * lax.rsqrt(var + RW_GN_EPS) * gn_w + gn_b
    y = y + headsum(r * k2 * r_k) * v
    y_ref[0] = (y * g).astype(BF16)


def _rwkv(rkv, small, vfirst_rkv, lw, la, lg, lv, vecs, ones_bd, *, n_chunks=4):
    B, T, _ = rkv.shape
    C = n_chunks * RW_CHUNK
    has_vres = vfirst_rkv is not None
    row3 = lambda b, j: (b, j, 0)
    in_specs = [pl.BlockSpec((1, C, 3 * WIDTH), row3), pl.BlockSpec((1, C, SMALL_W), row3)]
    args = [rkv, small]
    if has_vres:
        in_specs.append(pl.BlockSpec((1, C, WIDTH), lambda b, j: (b, j, 2)))
        args.append(vfirst_rkv)
    w_list = [lw, la, lg] + ([lv] if has_vres else []) + [vecs, ones_bd]
    in_specs += [_const_spec(t.shape) for t in w_list]
    args += w_list
    return pl.pallas_call(
        functools.partial(_rwkv_body, has_vres=has_vres, n_chunks=n_chunks),
        out_shape=jax.ShapeDtypeStruct((B, T, WIDTH), BF16),
        grid=(B, T // C),
        in_specs=in_specs,
        out_specs=pl.BlockSpec((1, C, WIDTH), row3),
        scratch_shapes=[pltpu.VMEM((N_HEADS // RW_GROUP, GW, GW), F32)],
        compiler_params=_cparams(("arbitrary", "arbitrary")),
        name="rwkv",
    )(*args)


def _mla_prep_body(lat_ref, small_ref, pos_ref, gq_ref, gkv_ref, wq_ref, wk_ref, wvt_ref, freq_ref,
                   q_ref, k_ref, vt_ref):
    def rms(t, gain):
        return t * lax.rsqrt(jnp.mean(t * t, axis=-1, keepdims=True) + NORM_EPS) * gain

    lat = lat_ref[0]
    qn = rms(lat[:, :MLA_LORA], gq_ref[...]).astype(BF16)
    kvn = rms(lat[:, MLA_LORA:], gkv_ref[...]).astype(BF16)
    q = jnp.dot(qn, wq_ref[...], preferred_element_type=F32)
    kn = jnp.dot(kvn, wk_ref[...], preferred_element_type=F32)
    vt = lax.dot_general(wvt_ref[...], kvn, (((1,), (1,)), ((), ())), preferred_element_type=F32)

    ang = pos_ref[0].astype(F32) * freq_ref[...]
    cos, sin = jnp.cos(ang), jnp.sin(ang)
    half = MLA_ROPE // 2
    q1, q2 = q[:, WIDTH:WIDTH + 128], q[:, WIDTH + 128:WIDTH + 256]
    qo1 = q1 * cos - q2 * sin
    qo2 = q1 * sin + q2 * cos
    kr = small_ref[0][:, S_KR:S_KR + MLA_ROPE]
    k1, k2 = kr[:, :half], kr[:, half:]
    c16, s16 = cos[:, :half], sin[:, :half]
    ko = jnp.concatenate([k1 * c16 - k2 * s16, k1 * s16 + k2 * c16], axis=-1)

    scale = MLA_QK ** -0.5 * LOG2E
    for hd in range(N_HEADS):
        hs = slice(hd * HEAD_DIM, (hd + 1) * HEAD_DIM)
        ps = slice(hd * half, (hd + 1) * half)
        qh = jnp.concatenate([q[:, hs], qo1[:, ps], qo2[:, ps]], axis=-1) * scale
        q_ref[0, hd] = qh.astype(BF16)
        k_ref[0, hd] = jnp.concatenate([kn[:, hs], ko], axis=-1).astype(BF16)
        vt_ref[0, hd, 0:HEAD_DIM, :] = vt[hs, :].astype(BF16)
        vt_ref[0, hd, HEAD_DIM:VT_ROWS, :] = _ones_rows(vt.shape[1])


def _mla_prep(lat, small, positions, gq, gkv, wq, wk, wv_t, freq, *, tm=512):
    B, T, _ = lat.shape
    row3 = lambda b, j: (b, j, 0)
    hm = lambda d: pl.BlockSpec((1, N_HEADS, tm, d), lambda b, j: (b, 0, j, 0))
    return pl.pallas_call(
        _mla_prep_body,
        out_shape=(jax.ShapeDtypeStruct((B, N_HEADS, T, MLA_QK), BF16),
                   jax.ShapeDtypeStruct((B, N_HEADS, T, MLA_QK), BF16),
                   jax.ShapeDtypeStruct((B, N_HEADS, VT_ROWS, T), BF16)),
        grid=(B, T // tm),
        in_specs=[pl.BlockSpec((1, tm, 2 * MLA_LORA), row3),
                  pl.BlockSpec((1, tm, SMALL_W), row3),
                  pl.BlockSpec((1, tm, 1), row3),
                  _const_spec(gq.shape), _const_spec(gkv.shape),
                  _const_spec(wq.shape), _const_spec(wk.shape), _const_spec(wv_t.shape),
                  _const_spec(freq.shape)],
        out_specs=(hm(MLA_QK), hm(MLA_QK),
                   pl.BlockSpec((1, N_HEADS, VT_ROWS, tm), lambda b, j: (b, 0, 0, j))),
        compiler_params=_cparams(("arbitrary", "arbitrary")),
        name="mla_prep",
    )(lat, small, positions, gq, gkv, wq, wk, wv_t, freq)


def _attn_body(q_ref, k_ref, vt_ref, o_ref, m_ref, acc_ref, *, tq, tks):
    qi = pl.program_id(2)
    tk = tq
    nsub = tk // tks
    hps = q_ref.shape[1]

    m_ref[...] = jnp.full_like(m_ref, NEG_BIG)
    acc_ref[...] = jnp.zeros_like(acc_ref)

    def step(ki, diag):
        k0 = pl.multiple_of(ki * tk, tk)
        units = [(i, s) for s in range(nsub) for i in range(hps)]

        def lo_of(s):
            return s * tks if diag else 0

        def qk(i, s):
            kb = k_ref[0, i, pl.ds(k0 + s * tks, tks), :]
            st = lax.dot_general(kb, q_ref[0, i, lo_of(s):, :], (((1,), (1,)), ((), ())),
                                 preferred_element_type=F32)
            if diag:
                key = lax.broadcasted_iota(jnp.int32, st.shape, 0)
                qry = lax.broadcasted_iota(jnp.int32, st.shape, 1)
                st = jnp.where(key <= qry, st, NEG_BIG)
            return st

        def softmax(i, s, st):
            lo = lo_of(s)
            m_old = m_ref[i, :, lo:]
            m_new = jnp.maximum(m_old, jnp.max(st, axis=0, keepdims=True))
            m_ref[i, :, lo:] = m_new
            return jnp.exp2(st - m_new).astype(BF16), jnp.exp2(m_old - m_new)

        def pv(i, s, p, alpha):
            lo = lo_of(s)
            vtb = vt_ref[0, i, :, pl.ds(k0 + s * tks, tks)]
            acc_ref[i, :, lo:] = alpha * acc_ref[i, :, lo:] + jnp.dot(vtb, p, preferred_element_type=F32)

        n = len(units)
        sc = {u: qk(*units[u]) for u in range(min(2, n))}
        pr = {}
        for u in range(n):
            pr[u] = softmax(*units[u], sc.pop(u))
            if u + 2 < n:
                sc[u + 2] = qk(*units[u + 2])
            if u >= 1:
                pv(*units[u - 1], *pr.pop(u - 1))
        pv(*units[n - 1], *pr.pop(n - 1))

    def loop_body(ki, carry):
        step(ki, False)
        return carry

    lax.fori_loop(0, qi, loop_body, 0)
    step(qi, True)

    out_t = jnp.concatenate([acc_ref[i, 0:HEAD_DIM, :] / acc_ref[i, HEAD_DIM:HEAD_DIM + 1, :]
                             for i in range(hps)], axis=0)
    o_ref[0] = out_t.T.astype(BF16)


def _attn(q, k, vt, *, name, tq=512, tks=256, hps=4):
    B, H, T, dk = q.shape
    nq = T // tq
    return pl.pallas_call(
        functools.partial(_attn_body, tq=tq, tks=tks),
        out_shape=jax.ShapeDtypeStruct((B, T, H * HEAD_DIM), BF16),
        grid=(B, H // hps, nq),
        in_specs=[pl.BlockSpec((1, hps, tq, dk), lambda b, p, i: (b, p, i, 0)),
                  pl.BlockSpec((1, hps, T, dk), lambda b, p, i: (b, p, 0, 0)),
                  pl.BlockSpec((1, hps, VT_ROWS, T), lambda b, p, i: (b, p, 0, 0))],
        out_specs=pl.BlockSpec((1, tq, hps * HEAD_DIM), lambda b, p, i: (b, i, p)),
        scratch_shapes=[pltpu.VMEM((hps, 1, tq), F32), pltpu.VMEM((hps, VT_ROWS, tq), F32)],
        compiler_params=_cparams(("arbitrary", "arbitrary", "arbitrary")),
        name=name,
    )(q, k, vt)


def _merge_body(yr_ref, yf_ref, ym_ref, gate_ref, x_ref, mod_ref, wb_ref, wo_ref, o_ref):
    D = D_MODEL
    mixed = None
    for gi, y_ref in enumerate((yr_ref, yf_ref, ym_ref)):
        bd = jnp.dot(y_ref[0], wb_ref[gi], preferred_element_type=F32)
        term = gate_ref[0, :, gi * D:(gi + 1) * D].astype(F32) * bd
        mixed = term if mixed is None else mixed + term
    out = jnp.dot(mixed.astype(BF16), wo_ref[...], preferred_element_type=F32)
    o_ref[0] = x_ref[0] + mod_ref[0, 2:3, :] * out


def _merge(y_rw, y_fox, y_mla, gates, x, mod, wb, wo, *, tm=512):
    B, T, D = x.shape
    row3 = lambda b, j: (b, j, 0)
    yspec = pl.BlockSpec((1, tm, WIDTH), row3)
    return pl.pallas_call(
        _merge_body,
        out_shape=jax.ShapeDtypeStruct((B, T, D), F32),
        grid=(B, T // tm),
        in_specs=[yspec, yspec, yspec,
                  pl.BlockSpec((1, tm, N_BRANCH * D), row3),
                  pl.BlockSpec((1, tm, D), row3),
                  pl.BlockSpec((1, 6, D), lambda b, j: (b, 0, 0)),
                  _const_spec(wb.shape), _const_spec(wo.shape)],
        out_specs=pl.BlockSpec((1, tm, D), row3),
        compiler_params=_cparams(("arbitrary", "arbitrary")),
        name="merge",
    )(y_rw, y_fox, y_mla, gates, x, mod, wb, wo)


def _ffn_body(x_ref, mod_ref, g_ref, wup_ref, cw_ref, cb_ref, wdn_ref, fg_ref, o_ref, carry_ref,
              *, tm, fc, final):
    j = pl.program_id(1)

    @pl.when(j == 0)
    def _():
        carry_ref[...] = jnp.zeros_like(carry_ref)

    x = x_ref[0]
    hb = _modulated_norm(x, g_ref[...], mod_ref[0, 3:4, :], mod_ref[0, 4:5, :]).astype(BF16)
    row8 = lax.broadcasted_iota(jnp.int32, (8, 1), 0)

    def up(c0):
        return jnp.dot(hb, wup_ref[:, c0:c0 + fc], preferred_element_type=F32)

    def conv(u, c0):
        t2 = carry_ref[0:1, c0:c0 + fc]
        t1 = carry_ref[1:2, c0:c0 + fc]
        carry_ref[0:2, c0:c0 + fc] = u[tm - 2:tm, :]
        r1 = pltpu.roll(u, 1, 0)
        r2 = pltpu.roll(u, 2, 0)
        p1 = jnp.concatenate([jnp.where(row8 == 0, t1, r1[:8]), r1[8:]], axis=0)
        p2 = jnp.concatenate([jnp.where(row8 == 0, t2, jnp.where(row8 == 1, t1, r2[:8])), r2[8:]], axis=0)
        return (u * cw_ref[2:3, c0:c0 + fc] + p1 * cw_ref[1:2, c0:c0 + fc]
                + p2 * cw_ref[0:1, c0:c0 + fc] + cb_ref[:, c0:c0 + fc])

    n_ff = D_FF // fc
    raw = {0: (up(0), up(D_FF))}
    acc = None
    for c in range(n_ff):
        if c + 1 < n_ff:
            raw[c + 1] = (up((c + 1) * fc), up(D_FF + (c + 1) * fc))
        rg, rv = raw.pop(c)
        ug = conv(rg, c * fc)
        uv = conv(rv, D_FF + c * fc)
        act = (ug * _sigmoid(ug) * uv).astype(BF16)
        part = jnp.dot(act, wdn_ref[c * fc:(c + 1) * fc, :], preferred_element_type=F32)
        acc = part if acc is None else acc + part
    out = x + mod_ref[0, 5:6, :] * acc
    if final:
        out = out * lax.rsqrt(jnp.mean(out * out, axis=-1, keepdims=True) + NORM_EPS) * fg_ref[...]
    o_ref[0] = out


def _ffn(x, mod, gain, w_up, conv_w, conv_b, w_down, final_g, *, final, tm=512, fc=256):
    B, T, D = x.shape
    row3 = lambda b, j: (b, j, 0)
    return pl.pallas_call(
        functools.partial(_ffn_body, tm=tm, fc=fc, final=final),
        out_shape=jax.ShapeDtypeStruct((B, T, D), F32),
        grid=(B, T // tm),
        in_specs=[pl.BlockSpec((1, tm, D), row3),
                  pl.BlockSpec((1, 6, D), lambda b, j: (b, 0, 0)),
                  _const_spec(gain.shape), _const_spec(w_up.shape), _const_spec(conv_w.shape),
                  _const_spec(conv_b.shape), _const_spec(w_down.shape), _const_spec(final_g.shape)],
        out_specs=pl.BlockSpec((1, tm, D), row3),
        scratch_shapes=[pltpu.VMEM((8, 2 * D_FF), F32)],
        compiler_params=_cparams(("arbitrary", "arbitrary")),
        name="ffn",
    )(x, mod, gain, w_up, conv_w, conv_b, w_down, final_g)


def _cat_weight(w_in_i, vres_down_i):
    D = w_in_i.shape[0]
    o_wl, o_k, o_v, o_al, o_gl, rw_cols = 512, 576, 1088, 1600, 1664, 1792
    o_fox = rw_cols
    o_ff = o_fox + 3 * WIDTH
    o_mq = o_ff + N_HEADS
    o_kr = o_mq + 2 * MLA_LORA
    o_gate = o_kr + MLA_ROPE
    vres = jnp.zeros((D, RW_V_LORA), w_in_i.dtype) if vres_down_i is None else vres_down_i
    pad = jnp.zeros((D, SMALL_W - (S_FF + N_HEADS)), w_in_i.dtype)
    cols = [w_in_i[:, 0:512], w_in_i[:, o_k:o_k + 512], w_in_i[:, o_v:o_v + 512],
            w_in_i[:, o_wl:o_wl + 64], w_in_i[:, o_al:o_al + 64], w_in_i[:, o_gl:o_gl + 128],
            vres, w_in_i[:, o_kr:o_kr + MLA_ROPE], w_in_i[:, o_ff:o_ff + N_HEADS], pad,
            w_in_i[:, o_fox:o_fox + 2 * WIDTH],
            w_in_i[:, o_mq:o_mq + 2 * MLA_LORA],
            w_in_i[:, o_gate:o_gate + N_BRANCH * D_MODEL]]
    w_fv_t = w_in_i[:, o_fox + 2 * WIDTH:o_fox + 3 * WIDTH].T
    return jnp.concatenate(cols, axis=1).astype(BF16), w_fv_t.astype(BF16)


def _cat_mu(mu_i, vres_mu_i):
    o_wl, o_k, o_v, o_al, o_gl = 512, 576, 1088, 1600, 1664
    vres = jnp.zeros((RW_V_LORA,), F32) if vres_mu_i is None else vres_mu_i
    tail = jnp.zeros((SMALL_W - S_KR,), F32)
    return jnp.concatenate([mu_i[0:512], mu_i[o_k:o_k + 512], mu_i[o_v:o_v + 512],
                            mu_i[o_wl:o_wl + 64], mu_i[o_al:o_al + 64], mu_i[o_gl:o_gl + 128],
                            vres, tail]).reshape(1, N_SHIFT)


def _mla_weights(w_q_up, w_kv_up):
    half = MLA_ROPE // 2
    wq = w_q_up.reshape(MLA_LORA, N_HEADS, MLA_QK)
    wq = jnp.concatenate([wq[:, :, :MLA_NOPE].reshape(MLA_LORA, -1),
                          wq[:, :, MLA_NOPE:MLA_NOPE + half].reshape(MLA_LORA, -1),
                          wq[:, :, MLA_NOPE + half:].reshape(MLA_LORA, -1)], axis=1)
    wkv = w_kv_up.reshape(MLA_LORA, N_HEADS, MLA_NOPE + HEAD_DIM)
    wk = wkv[:, :, :MLA_NOPE].reshape(MLA_LORA, -1)
    wv_t = wkv[:, :, MLA_NOPE:].reshape(MLA_LORA, -1).T
    return wq.astype(BF16), wk.astype(BF16), wv_t.astype(BF16)


def kernel(x, c, positions, norm1_g, norm2_g, w_ada, b_ada, w_in, mu_shift, rw_w_up, rw_w0, rw_a_up, rw_a0, rw_g_up, rw_k_k, rw_k_a, rw_r_k, rw_gn_w, rw_gn_b, rw_vres_down, rw_vres_mu, rw_vres_up, rw_v0, fox_b_f, mla_q_norm_g, mla_w_q_up, mla_kv_norm_g, mla_w_kv_up, w_branch, w_o, ffn_w_up, ffn_conv_w, ffn_conv_b, ffn_w_down, final_g):
    B, T, D = x.shape
    depth = w_in.shape[0]
    half = MLA_ROPE // 2
    inv_freq = np.power(np.float32(ROPE_BASE), -np.arange(half, dtype=np.float32) / np.float32(half))
    freq = jnp.asarray(np.tile(inv_freq.astype(np.float32), 128 // half).reshape(1, 128))
    hid = np.arange(WIDTH // 2) // HEAD_DIM
    ones_bd = jnp.asarray((hid[:, None] == hid[None, :]).astype(np.float32), BF16)
    pos3 = positions.reshape(B, T, 1)

    mods = _ada(c, w_ada, b_ada).reshape(depth, B, 6, D)
    rkv0 = None
    for i in range(depth):
        mod = mods[i]
        w_cat, w_fv_t = _cat_weight(w_in[i], None if i == 0 else rw_vres_down[i - 1])
        mu = _cat_mu(mu_shift[i], None if i == 0 else rw_vres_mu[i - 1])
        rkv, small, fq, fk, fvt, lat, gates = _proj(
            x, mod, norm1_g[i].reshape(1, D), w_cat, w_fv_t, mu, fox_b_f[i].reshape(1, N_HEADS))

        zero = jnp.zeros((WIDTH,), F32)
        vecs = jnp.stack([rw_w0[i], rw_a0[i], rw_k_k[i], rw_k_a[i], rw_r_k[i], rw_gn_w[i], rw_gn_b[i],
                          zero if i == 0 else rw_v0[i - 1]])
        y_rw = _rwkv(rkv, small, None if i == 0 else rkv0,
                     rw_w_up[i].astype(BF16), rw_a_up[i].astype(BF16), rw_g_up[i].astype(BF16),
                     None if i == 0 else rw_vres_up[i - 1].astype(BF16), vecs, ones_bd)
        if i == 0:
            rkv0 = rkv

        y_fox = _attn(fq, fk, fvt, name="attn_fox")

        wq, wk, wv_t = _mla_weights(mla_w_q_up[i], mla_w_kv_up[i])
        mq, mk, mvt = _mla_prep(lat, small, pos3, mla_q_norm_g[i].reshape(1, -1),
                                mla_kv_norm_g[i].reshape(1, -1), wq, wk, wv_t, freq)
        y_mla = _attn(mq, mk, mvt, name="attn_mla")

        x = _merge(y_rw, y_fox, y_mla, gates, x, mod,
                   w_branch[i].astype(BF16), w_o[i].astype(BF16))
        x = _ffn(x, mod, norm2_g[i].reshape(1, D), ffn_w_up[i].astype(BF16), ffn_conv_w[i],
                 ffn_conv_b[i].reshape(1, -1), ffn_w_down[i].astype(BF16), final_g.reshape(1, D),
                 final=(i == depth - 1))
    return x
```

```python
import functools

import numpy as np
import jax
import jax.numpy as jnp
from jax import lax
from jax.experimental import pallas as pl
from jax.experimental.pallas import tpu as pltpu

F32 = jnp.float32
BF16 = jnp.bfloat16
HIGHEST = lax.Precision.HIGHEST

D_MODEL = 1024
HEAD_DIM = 64
N_HEADS = 8
WIDTH = N_HEADS * HEAD_DIM
RW_DECAY_LORA = 64
RW_A_LORA = 64
RW_V_LORA = 32
RW_G_LORA = 128
RW_GN_EPS = 64e-5
MLA_NOPE = 64
MLA_ROPE = 32
MLA_QK = MLA_NOPE + MLA_ROPE
MLA_LORA = 256
ROPE_BASE = 10000.0
N_BRANCH = 3
D_FF = 2816
NORM_EPS = 1e-6
NEG_BIG = -1e30

VMEM_LIMIT_BYTES = 56 * 1024 * 1024

C_RKV = 0
C_SMALL = 3 * WIDTH
SMALL_W = 384
S_WL, S_AL, S_GL, S_VR, S_KR, S_FF = 0, 64, 128, 256, 288, 320
C_FOX = C_SMALL + SMALL_W
C_LAT = C_FOX + 2 * WIDTH
C_GATE = C_LAT + 2 * MLA_LORA
N_COLS = C_GATE + N_BRANCH * D_MODEL
N_SHIFT = C_FOX

LOG2E = 1.4426950408889634
FOX_EXTRA = 8
FOX_QK = HEAD_DIM + FOX_EXTRA
VT_ROWS = HEAD_DIM + 16


def _ones_rows(n):
    r = lax.broadcasted_iota(jnp.int32, (VT_ROWS - HEAD_DIM, n), 0)
    return jnp.where(r == 0, 1.0, 0.0).astype(BF16)

RW_CHUNK = 64
RW_GROUP = 2
GW = RW_GROUP * HEAD_DIM


def _cparams(sem):
    return pltpu.CompilerParams(dimension_semantics=sem, vmem_limit_bytes=VMEM_LIMIT_BYTES)


def _const_spec(shape):
    n = len(shape)
    return pl.BlockSpec(shape, lambda *_: (0,) * n, pipeline_mode=pl.Buffered(1))


def _sigmoid(z):
    return 1.0 / (1.0 + jnp.exp(-z))


def _softplus(z):
    return jnp.maximum(z, 0.0) + jnp.log1p(jnp.exp(-jnp.abs(z)))


def _modulated_norm(x, gain, shift, scale):
    ms = jnp.mean(x * x, axis=-1, keepdims=True)
    return (x * lax.rsqrt(ms + NORM_EPS) * gain) * (1.0 + scale) + shift


def _ada_body(c_ref, w_ref, b_ref, o_ref):
    c = c_ref[...]
    ca = c * _sigmoid(c)
    o_ref[0] = jnp.dot(ca, w_ref[0], precision=HIGHEST, preferred_element_type=F32) + b_ref[0]


def _ada(c, w_ada, b_ada):
    L, D, N6 = w_ada.shape
    B = c.shape[0]
    tn = 1536
    return pl.pallas_call(
        _ada_body,
        out_shape=jax.ShapeDtypeStruct((L, B, N6), F32),
        grid=(L, N6 // tn),
        in_specs=[pl.BlockSpec((B, D), lambda l, n: (0, 0)),
                  pl.BlockSpec((1, D, tn), lambda l, n: (l, 0, n)),
                  pl.BlockSpec((1, 1, tn), lambda l, n: (l, 0, n))],
        out_specs=pl.BlockSpec((1, B, tn), lambda l, n: (l, 0, n)),
        compiler_params=_cparams(("arbitrary", "arbitrary")),
        name="ada",
    )(c, w_ada, b_ada.reshape(L, 1, N6))


def _bf16_round(t):
    return t.astype(BF16).astype(F32)


def _proj_body(x_ref, mod_ref, g_ref, w_ref, wvt_ref, mu_ref, bf_ref,
               rkv_ref, small_ref, fq_ref, fk_ref, fvt_ref, lat_ref, gate_ref,
               carry_ref, cc_ref, *, tm):
    j = pl.program_id(1)

    @pl.when(j == 0)
    def _():
        carry_ref[...] = jnp.zeros_like(carry_ref)
        cc_ref[...] = jnp.zeros_like(cc_ref)

    h = _modulated_norm(x_ref[0], g_ref[...], mod_ref[0, 0:1, :], mod_ref[0, 1:2, :])
    hb = h.astype(BF16)
    row = lax.broadcasted_iota(jnp.int32, (tm, 1), 0)

    def mm(c0, c1):
        return jnp.dot(hb, w_ref[:, c0:c1], preferred_element_type=F32)

    def shifted(c0, c1):
        res = mm(c0, c1)
        last = carry_ref[0:1, c0:c1]
        carry_ref[0:1, c0:c1] = res[tm - 1:tm, :]
        prev = jnp.where(row == 0, last, pltpu.roll(res, 1, 0))
        return res + (prev - res) * mu_ref[:, c0:c1]

    for i in range(3):
        c0 = C_RKV + i * WIDTH
        rkv_ref[0, :, c0:c0 + WIDTH] = shifted(c0, c0 + WIDTH).astype(BF16)
    small = shifted(C_SMALL, C_SMALL + SMALL_W)
    small_ref[0] = small

    z = small[:, S_FF:S_FF + N_HEADS] + bf_ref[...]
    logf = jnp.minimum(z, 0.0) - jnp.log1p(jnp.exp(-jnp.abs(z)))
    ri = lax.broadcasted_iota(jnp.int32, (tm, tm), 0)
    ci = lax.broadcasted_iota(jnp.int32, (tm, tm), 1)
    cum = jnp.dot((ci <= ri).astype(F32), logf, precision=HIGHEST,
                  preferred_element_type=F32) + cc_ref[0:1, 0:N_HEADS]
    cc_ref[0:1, 0:N_HEADS] = cum[tm - 1:tm, :]
    cum2 = cum * LOG2E

    fq = mm(C_FOX, C_FOX + WIDTH) * (HEAD_DIM ** -0.5 * LOG2E)
    fk = mm(C_FOX + WIDTH, C_FOX + 2 * WIDTH)
    lane = lax.broadcasted_iota(jnp.int32, (tm, FOX_EXTRA), 1)
    for hd in range(N_HEADS):
        c8 = jnp.broadcast_to(cum2[:, hd:hd + 1], (tm, FOX_EXTRA))
        hi = _bf16_round(c8)
        mid = _bf16_round(c8 - hi)
        lo = _bf16_round(c8 - hi - mid)
        eq = jnp.where(lane == 0, hi, jnp.where(lane == 1, mid, jnp.where(lane == 2, lo,
                       jnp.where(lane < 6, 1.0, 0.0))))
        ek = jnp.where(lane < 3, 1.0, jnp.where(lane == 3, -hi, jnp.where(lane == 4, -mid,
                       jnp.where(lane == 5, -lo, 0.0))))
        hs = slice(hd * HEAD_DIM, (hd + 1) * HEAD_DIM)
        fq_ref[0, hd] = jnp.concatenate([fq[:, hs], eq], axis=-1).astype(BF16)
        fk_ref[0, hd] = jnp.concatenate([fk[:, hs], ek], axis=-1).astype(BF16)

    fvt = lax.dot_general(wvt_ref[...], hb, (((1,), (1,)), ((), ())), preferred_element_type=F32)
    for hd in range(N_HEADS):
        fvt_ref[0, hd, 0:HEAD_DIM, :] = fvt[hd * HEAD_DIM:(hd + 1) * HEAD_DIM, :].astype(BF16)
        fvt_ref[0, hd, HEAD_DIM:VT_ROWS, :] = _ones_rows(tm)

    lat_ref[0] = mm(C_LAT, C_GATE)

    for i in range(N_BRANCH * D_MODEL // WIDTH):
        c0 = C_GATE + i * WIDTH
        gate_ref[0, :, i * WIDTH:(i + 1) * WIDTH] = _sigmoid(mm(c0, c0 + WIDTH)).astype(BF16)


def _proj(x, mod, gain, w_cat, w_fv_t, mu, b_f, *, tm=256):
    B, T, D = x.shape
    nt = T // tm
    qk = jax.ShapeDtypeStruct((B, N_HEADS, T, FOX_QK), BF16)
    qk_spec = pl.BlockSpec((1, N_HEADS, tm, FOX_QK), lambda b, j: (b, 0, j, 0))
    row3 = lambda b, j: (b, j, 0)
    return pl.pallas_call(
        functools.partial(_proj_body, tm=tm),
        out_shape=(jax.ShapeDtypeStruct((B, T, 3 * WIDTH), BF16),
                   jax.ShapeDtypeStruct((B, T, SMALL_W), F32),
                   qk, qk,
                   jax.ShapeDtypeStruct((B, N_HEADS, VT_ROWS, T), BF16),
                   jax.ShapeDtypeStruct((B, T, 2 * MLA_LORA), F32),
                   jax.ShapeDtypeStruct((B, T, N_BRANCH * D), BF16)),
        grid=(B, nt),
        in_specs=[pl.BlockSpec((1, tm, D), row3),
                  pl.BlockSpec((1, 6, D), lambda b, j: (b, 0, 0)),
                  _const_spec((1, D)),
                  _const_spec((D, N_COLS)),
                  _const_spec((WIDTH, D)),
                  _const_spec((1, N_SHIFT)),
                  _const_spec((1, N_HEADS))],
        out_specs=(pl.BlockSpec((1, tm, 3 * WIDTH), row3),
                   pl.BlockSpec((1, tm, SMALL_W), row3),
                   qk_spec, qk_spec,
                   pl.BlockSpec((1, N_HEADS, VT_ROWS, tm), lambda b, j: (b, 0, 0, j)),
                   pl.BlockSpec((1, tm, 2 * MLA_LORA), row3),
                   pl.BlockSpec((1, tm, N_BRANCH * D), row3)),
        scratch_shapes=[pltpu.VMEM((8, N_SHIFT), F32),
                        pltpu.VMEM((8, 128), F32)],
        compiler_params=_cparams(("arbitrary", "arbitrary")),
        name="proj",
    )(x, mod, gain, w_cat, w_fv_t, mu, b_f)


def _rwkv_body(*refs, has_vres, n_chunks):
    if has_vres:
        (rkv_ref, small_ref, vfirst_ref, lw_ref, la_ref, lg_ref, lv_ref, vec_ref, ones_ref,
         y_ref, ht_ref) = refs
    else:
        rkv_ref, small_ref, lw_ref, la_ref, lg_ref, vec_ref, ones_ref, y_ref, ht_ref = refs
    C = RW_CHUNK
    TC = n_chunks * C
    j = pl.program_id(1)

    @pl.when(j == 0)
    def _():
        ht_ref[...] = jnp.zeros_like(ht_ref)

    def vec(i):
        return vec_ref[i:i + 1, :]

    w0, a0, k_k, k_a, r_k, gn_w, gn_b, v0 = (vec(i) for i in range(8))
    ones_bd = ones_ref[...]
    HW = ones_bd.shape[0]

    def headsum(t):
        tb = t.astype(BF16)
        return jnp.concatenate([jnp.dot(tb[:, c0:c0 + HW], ones_bd, preferred_element_type=F32)
                                for c0 in range(0, WIDTH, HW)], axis=1)

    def lora(t, w_ref):
        return jnp.dot(t.astype(BF16), w_ref[...], preferred_element_type=F32)

    r = rkv_ref[0, :, 0:WIDTH].astype(F32)
    k = rkv_ref[0, :, WIDTH:2 * WIDTH].astype(F32)
    v = rkv_ref[0, :, 2 * WIDTH:3 * WIDTH].astype(F32)
    sm = small_ref[0]
    wl = sm[:, S_WL:S_WL + RW_DECAY_LORA]
    al = sm[:, S_AL:S_AL + RW_A_LORA]
    gl = sm[:, S_GL:S_GL + RW_G_LORA]

    w = -_softplus(-(w0 + lora(jnp.tanh(wl), lw_ref))) - 0.5
    logd = -jnp.exp(w)
    a = _sigmoid(a0 + lora(al, la_ref))
    g = lora(_sigmoid(gl), lg_ref)
    if has_vres:
        lvv = sm[:, S_VR:S_VR + RW_V_LORA]
        v = v + (vfirst_ref[0].astype(F32) - v) * _sigmoid(v0 + lora(lvv, lv_ref))
    kk = k * k_k
    kk = kk * lax.rsqrt(jnp.maximum(headsum(kk * kk), 1e-24))
    k2 = k * (1.0 + (a - 1.0) * k_a)

    ri = lax.broadcasted_iota(jnp.int32, (TC, TC), 0)
    ci = lax.broadcasted_iota(jnp.int32, (TC, TC), 1)
    tri = jnp.where((ci <= ri) & ((ri // C) == (ci // C)), 1.0, 0.0).astype(BF16)
    ld_hi = logd.astype(BF16)
    ld_mid = (logd - ld_hi.astype(F32)).astype(BF16)
    ld_lo = (logd - ld_hi.astype(F32) - ld_mid.astype(F32)).astype(BF16)
    cl = (jnp.dot(tri, ld_hi, preferred_element_type=F32) + jnp.dot(tri, ld_mid, preferred_element_type=F32)
          + jnp.dot(tri, ld_lo, preferred_element_type=F32))
    cl_last = jnp.concatenate([jnp.broadcast_to(cl[(c + 1) * C - 1:(c + 1) * C, :], (C, WIDTH))
                               for c in range(n_chunks)], axis=0)
    e_neg = jnp.exp(-cl)
    e_end = jnp.exp(cl_last - cl)
    ka = kk * a
    a_t = kk * jnp.exp(cl - logd)
    r_t = r * jnp.exp(cl)
    b_m = -ka * e_neg
    k_m = k2 * e_neg
    b_p = -ka * e_end
    k_p = k2 * e_end
    p_c = jnp.exp(cl_last)

    bi = lax.broadcasted_iota(jnp.int32, (GW, GW), 0)
    bj = lax.broadcasted_iota(jnp.int32, (GW, GW), 1)
    bd_mask = (bi // HEAD_DIM) == (bj // HEAD_DIM)
    strict = bi > bj
    incl = bi >= bj
    eye = (bi == bj).astype(F32)

    def bd(t):
        return jnp.where(bd_mask, jnp.concatenate([t] * RW_GROUP, axis=0), 0.0)

    def collapse(t):
        return sum(t[hh * C:(hh + 1) * C, :] for hh in range(RW_GROUP))

    def dot16(p, q):
        return jnp.dot(p.astype(BF16), q.astype(BF16), preferred_element_type=F32)

    chains = [(ch, gi) for ch in range(n_chunks) for gi in range(N_HEADS // RW_GROUP)]
    nc = len(chains)

    def piece(t, c):
        ch, gi = chains[c]
        return t[ch * C:(ch + 1) * C, gi * GW:(gi + 1) * GW]

    def bds(t):
        return [bd(piece(t, c)).astype(BF16) for c in range(nc)]

    at_bd, rt_bd, bm_bd, km_bd, bp_bd, kp_bd, v_bd = (bds(t) for t in (a_t, r_t, b_m, k_m, b_p, k_p, v))
    a_all = [lax.dot_general(jnp.concatenate([at_bd[c], rt_bd[c]], axis=0),
                             jnp.concatenate([bm_bd[c], km_bd[c]], axis=0),
                             (((1,), (1,)), ((), ())), preferred_element_type=F32) for c in range(nc)]
    a_ab = [jnp.where(strict, a_all[c][:GW, :GW], 0.0) for c in range(nc)]
    a_ak = [jnp.where(strict, a_all[c][:GW, GW:], 0.0).astype(BF16) for c in range(nc)]
    m_rb = [jnp.where(incl, a_all[c][GW:, :GW], 0.0).astype(BF16) for c in range(nc)]
    m_rk = [jnp.where(incl, a_all[c][GW:, GW:], 0.0).astype(BF16) for c in range(nc)]

    pw = [t.astype(BF16) for t in a_ab]
    tinv = [eye + t for t in a_ab]
    n_sq = int(np.log2(C)) - 1
    for it in range(n_sq):
        pw = [jnp.dot(pw[c], pw[c], preferred_element_type=F32).astype(BF16) for c in range(nc)]
        tinv = [tinv[c] + dot16(tinv[c], pw[c]) for c in range(nc)]
    tinv = [t.astype(BF16) for t in tinv]

    akv = [dot16(a_ak[c], v_bd[c]) for c in range(nc)]
    x = [dot16(tinv[c], jnp.concatenate([at_bd[c], akv[c].astype(BF16)], axis=1)) for c in range(nc)]
    wm_bd = [t[:, :GW].astype(BF16) for t in x]
    u0_bd = [t[:, GW:].astype(BF16) for t in x]
    y1_bd = [dot16(jnp.concatenate([m_rk[c], m_rb[c]], axis=1),
                   jnp.concatenate([v_bd[c], u0_bd[c]], axis=0)) for c in range(nc)]
    rm_bd = [dot16(m_rb[c], wm_bd[c]) for c in range(nc)]
    qt = [dot16(x[c][:, :GW].T, bp_bd[c]) for c in range(nc)]
    h1t = [dot16(jnp.concatenate([x[c][:, GW:], v_bd[c].astype(F32)], axis=0).T,
                 jnp.concatenate([bp_bd[c], kp_bd[c]], axis=0)) for c in range(nc)]
    local = {}
    for c, (ch, gi) in enumerate(chains):
        local[ch, gi] = (collapse(y1_bd[c]), (piece(r_t, c) + collapse(rm_bd[c])).astype(BF16),
                         qt[c].astype(BF16), h1t[c], p_c[ch * C:ch * C + 1, gi * GW:(gi + 1) * GW])

    y_rows = []
    for ch in range(n_chunks):
        ys = []
        for gi in range(N_HEADS // RW_GROUP):
            y1, rm, qt, h1t, pc = local[ch, gi]
            ht = ht_ref[gi]
            htb = ht.astype(BF16)
            ys.append(y1 + lax.dot_general(rm, htb, (((1,), (1,)), ((), ())), preferred_element_type=F32))
            ht_ref[gi] = ht * pc + jnp.dot(htb, qt, preferred_element_type=F32) + h1t
        y_rows.append(jnp.concatenate(ys, axis=1))
    y = jnp.concatenate(y_rows, axis=0)
    inv_n = 1.0 / HEAD_DIM
    mean = headsum(y) * inv_n
    dlt = y - mean
    var = headsum(dlt * dlt) * inv_n
    y = dlt * lax.rsqrt(var + RW_GN_EPS) * gn_w + gn_b
    y = y + headsum(r * k2 * r_k) * v
    y_ref[0] = (y * g).astype(BF16)


def _rwkv(rkv, small, vfirst_rkv, lw, la, lg, lv, vecs, ones_bd, *, n_chunks=4):
    B, T, _ = rkv.shape
    C = n_chunks * RW_CHUNK
    has_vres = vfirst_rkv is not None
    row3 = lambda b, j: (b, j, 0)
    in_specs = [pl.BlockSpec((1, C, 3 * WIDTH), row3), pl.BlockSpec((1, C, SMALL_W), row3)]
    args = [rkv, small]
    if has_vres:
        in_specs.append(pl.BlockSpec((1, C, WIDTH), lambda b, j: (b, j, 2)))
        args.append(vfirst_rkv)
    w_list = [lw, la, lg] + ([lv] if has_vres else []) + [vecs, ones_bd]
    in_specs += [_const_spec(t.shape) for t in w_list]
    args += w_list
    return pl.pallas_call(
        functools.partial(_rwkv_body, has_vres=has_vres, n_chunks=n_chunks),
        out_shape=jax.ShapeDtypeStruct((B, T, WIDTH), BF16),
        grid=(B, T // C),
        in_specs=in_specs,
        out_specs=pl.BlockSpec((1, C, WIDTH), row3),
        scratch_shapes=[pltpu.VMEM((N_HEADS // RW_GROUP, GW, GW), F32)],
        compiler_params=_cparams(("arbitrary", "arbitrary")),
        name="rwkv",
    )(*args)


def _mla_prep_body(lat_ref, small_ref, pos_ref, gq_ref, gkv_ref, wq_ref, wk_ref, wvt_ref, freq_ref,
                   q_ref, k_ref, vt_ref):
    def rms(t, gain):
        return t * lax.rsqrt(jnp.mean(t * t, axis=-1, keepdims=True) + NORM_EPS) * gain

    lat = lat_ref[0]
    qn = rms(lat[:, :MLA_LORA], gq_ref[...]).astype(BF16)
    kvn = rms(lat[:, MLA_LORA:], gkv_ref[...]).astype(BF16)
    q = jnp.dot(qn, wq_ref[...], preferred_element_type=F32)
    kn = jnp.dot(kvn, wk_ref[...], preferred_element_type=F32)
    vt = lax.dot_general(wvt_ref[...], kvn, (((1,), (1,)), ((), ())), preferred_element_type=F32)

    ang = pos_ref[0].astype(F32) * freq_ref[...]
    cos, sin = jnp.cos(ang), jnp.sin(ang)
    half = MLA_ROPE // 2
    q1, q2 = q[:, WIDTH:WIDTH + 128], q[:, WIDTH + 128:WIDTH + 256]
    qo1 = q1 * cos - q2 * sin
    qo2 = q1 * sin + q2 * cos
    kr = small_ref[0][:, S_KR:S_KR + MLA_ROPE]
    k1, k2 = kr[:, :half], kr[:, half:]
    c16, s16 = cos[:, :half], sin[:, :half]
    ko = jnp.concatenate([k1 * c16 - k2 * s16, k1 * s16 + k2 * c16], axis=-1)

    scale = MLA_QK ** -0.5 * LOG2E
    for hd in range(N_HEADS):
        hs = slice(hd * HEAD_DIM, (hd + 1) * HEAD_DIM)
        ps = slice(hd * half, (hd + 1) * half)
        qh = jnp.concatenate([q[:, hs], qo1[:, ps], qo2[:, ps]], axis=-1) * scale
        q_ref[0, hd] = qh.astype(BF16)
        k_ref[0, hd] = jnp.concatenate([kn[:, hs], ko], axis=-1).astype(BF16)
        vt_ref[0, hd, 0:HEAD_DIM, :] = vt[hs, :].astype(BF16)
        vt_ref[0, hd, HEAD_DIM:VT_ROWS, :] = _ones_rows(vt.shape[1])


def _mla_prep(lat, small, positions, gq, gkv, wq, wk, wv_t, freq, *, tm=512):
    B, T, _ = lat.shape
    row3 = lambda b, j: (b, j, 0)
    hm = lambda d: pl.BlockSpec((1, N_HEADS, tm, d), lambda b, j: (b, 0, j, 0))
    return pl.pallas_call(
        _mla_prep_body,
        out_shape=(jax.ShapeDtypeStruct((B, N_HEADS, T, MLA_QK), BF16),
                   jax.ShapeDtypeStruct((B, N_HEADS, T, MLA_QK), BF16),
                   jax.ShapeDtypeStruct((B, N_HEADS, VT_ROWS, T), BF16)),
        grid=(B, T // tm),
        in_specs=[pl.BlockSpec((1, tm, 2 * MLA_LORA), row3),
                  pl.BlockSpec((1, tm, SMALL_W), row3),
                  pl.BlockSpec((1, tm, 1), row3),
                  _const_spec(gq.shape), _const_spec(gkv.shape),
                  _const_spec(wq.shape), _const_spec(wk.shape), _const_spec(wv_t.shape),
                  _const_spec(freq.shape)],
        out_specs=(hm(MLA_QK), hm(MLA_QK),
                   pl.BlockSpec((1, N_HEADS, VT_ROWS, tm), lambda b, j: (b, 0, 0, j))),
        compiler_params=_cparams(("arbitrary", "arbitrary")),
        name="mla_prep",
    )(lat, small, positions, gq, gkv, wq, wk, wv_t, freq)


def _attn_body(q_ref, k_ref, vt_ref, o_ref, m_ref, acc_ref, *, tq, tks):
    qi = pl.program_id(2)
    tk = tq
    nsub = tk // tks
    hps = q_ref.shape[1]

    m_ref[...] = jnp.full_like(m_ref, NEG_BIG)
    acc_ref[...] = jnp.zeros_like(acc_ref)

    def step(ki, diag):
        k0 = pl.multiple_of(ki * tk, tk)
        units = [(i, s) for s in range(nsub) for i in range(hps)]

        def lo_of(s):
            return s * tks if diag else 0

        def qk(i, s):
            kb = k_ref[0, i, pl.ds(k0 + s * tks, tks), :]
            st = lax.dot_general(kb, q_ref[0, i, lo_of(s):, :], (((1,), (1,)), ((), ())),
                                 preferred_element_type=F32)
            if diag:
                key = lax.broadcasted_iota(jnp.int32, st.shape, 0)
                qry = lax.broadcasted_iota(jnp.int32, st.shape, 1)
                st = jnp.where(key <= qry, st, NEG_BIG)
            return st

        def softmax(i, s, st):
            lo = lo_of(s)
            m_old = m_ref[i, :, lo:]
            m_new = jnp.maximum(m_old, jnp.max(st, axis=0, keepdims=True))
            m_ref[i, :, lo:] = m_new
            return jnp.exp2(st - m_new).astype(BF16), jnp.exp2(m_old - m_new)

        def pv(i, s, p, alpha):
            lo = lo_of(s)
            vtb = vt_ref[0, i, :, pl.ds(k0 + s * tks, tks)]
            acc_ref[i, :, lo:] = alpha * acc_ref[i, :, lo:] + jnp.dot(vtb, p, preferred_element_type=F32)

        n = len(units)
        sc = {u: qk(*units[u]) for u in range(min(2, n))}
        pr = {}
        for u in range(n):
            pr[u] = softmax(*units[u], sc.pop(u))
            if u + 2 < n:
                sc[u + 2] = qk(*units[u + 2])
            if u >= 1:
                pv(*units[u - 1], *pr.pop(u - 1))
        pv(*units[n - 1], *pr.pop(n - 1))

    def loop_body(ki, carry):
        step(ki, False)
        return carry

    lax.fori_loop(0, qi, loop_body, 0)
    step(qi, True)

    out_t = jnp.concatenate([acc_ref[i, 0:HEAD_DIM, :] / acc_ref[i, HEAD_DIM:HEAD_DIM + 1, :]
                             for i in range(hps)], axis=0)
    o_ref[0] = out_t.T.astype(BF16)


def _attn(q, k, vt, *, name, tq=512, tks=256, hps=8):
    B, H, T, dk = q.shape
    nq = T // tq
    return pl.pallas_call(
        functools.partial(_attn_body, tq=tq, tks=tks),
        out_shape=jax.ShapeDtypeStruct((B, T, H * HEAD_DIM), BF16),
        grid=(B, H // hps, nq),
        in_specs=[pl.BlockSpec((1, hps, tq, dk), lambda b, p, i: (b, p, i, 0)),
                  pl.BlockSpec((1, hps, T, dk), lambda b, p, i: (b, p, 0, 0)),
                  pl.BlockSpec((1, hps, VT_ROWS, T), lambda b, p, i: (b, p, 0, 0))],
        out_specs=pl.BlockSpec((1, tq, hps * HEAD_DIM), lambda b, p, i: (b, i, p)),
        scratch_shapes=[pltpu.VMEM((hps, 1, tq), F32), pltpu.VMEM((hps, VT_ROWS, tq), F32)],
        compiler_params=_cparams(("arbitrary", "arbitrary", "arbitrary")),
        name=name,
    )(q, k, vt)


def _merge_body(yr_ref, yf_ref, ym_ref, gate_ref, x_ref, mod_ref, wb_ref, wo_ref, o_ref):
    D = D_MODEL
    mixed = None
    for gi, y_ref in enumerate((yr_ref, yf_ref, ym_ref)):
        bd = jnp.dot(y_ref[0], wb_ref[gi], preferred_element_type=F32)
        term = gate_ref[0, :, gi * D:(gi + 1) * D].astype(F32) * bd
        mixed = term if mixed is None else mixed + term
    out = jnp.dot(mixed.astype(BF16), wo_ref[...], preferred_element_type=F32)
    o_ref[0] = x_ref[0] + mod_ref[0, 2:3, :] * out


def _merge(y_rw, y_fox, y_mla, gates, x, mod, wb, wo, *, tm=512):
    B, T, D = x.shape
    row3 = lambda b, j: (b, j, 0)
    yspec = pl.BlockSpec((1, tm, WIDTH), row3)
    return pl.pallas_call(
        _merge_body,
        out_shape=jax.ShapeDtypeStruct((B, T, D), F32),
        grid=(B, T // tm),
        in_specs=[yspec, yspec, yspec,
                  pl.BlockSpec((1, tm, N_BRANCH * D), row3),
                  pl.BlockSpec((1, tm, D), row3),
                  pl.BlockSpec((1, 6, D), lambda b, j: (b, 0, 0)),
                  _const_spec(wb.shape), _const_spec(wo.shape)],
        out_specs=pl.BlockSpec((1, tm, D), row3),
        compiler_params=_cparams(("arbitrary", "arbitrary")),
        name="merge",
    )(y_rw, y_fox, y_mla, gates, x, mod, wb, wo)


def _ffn_body(x_ref, mod_ref, g_ref, wup_ref, cw_ref, cb_ref, wdn_ref, fg_ref, o_ref, carry_ref,
              *, tm, fc, final):
    j = pl.program_id(1)

    @pl.when(j == 0)
    def _():
        carry_ref[...] = jnp.zeros_like(carry_ref)

    x = x_ref[0]
    hb = _modulated_norm(x, g_ref[...], mod_ref[0, 3:4, :], mod_ref[0, 4:5, :]).astype(BF16)
    row8 = lax.broadcasted_iota(jnp.int32, (8, 1), 0)

    def up(c0):
        return jnp.dot(hb, wup_ref[:, c0:c0 + fc], preferred_element_type=F32)

    def conv(u, c0):
        t2 = carry_ref[0:1, c0:c0 + fc]
        t1 = carry_ref[1:2, c0:c0 + fc]
        carry_ref[0:2, c0:c0 + fc] = u[tm - 2:tm, :]
        r1 = pltpu.roll(u, 1, 0)
        r2 = pltpu.roll(u, 2, 0)
        p1 = jnp.concatenate([jnp.where(row8 == 0, t1, r1[:8]), r1[8:]], axis=0)
        p2 = jnp.concatenate([jnp.where(row8 == 0, t2, jnp.where(row8 == 1, t1, r2[:8])), r2[8:]], axis=0)
        return (u * cw_ref[2:3, c0:c0 + fc] + p1 * cw_ref[1:2, c0:c0 + fc]
                + p2 * cw_ref[0:1, c0:c0 + fc] + cb_ref[:, c0:c0 + fc])

    n_ff = D_FF // fc
    raw = {0: (up(0), up(D_FF))}
    acts = {}
    acc = None

    def down(c):
        part = jnp.dot(acts.pop(c), wdn_ref[c * fc:(c + 1) * fc, :], preferred_element_type=F32)
        return part if acc is None else acc + part

    for c in range(n_ff):
        rg, rv = raw.pop(c)
        if c + 1 < n_ff:
            nxt_g = up((c + 1) * fc)
        ug = conv(rg, c * fc)
        gate = ug * _sigmoid(ug)
        if c + 1 < n_ff:
            raw[c + 1] = (nxt_g, up(D_FF + (c + 1) * fc))
        uv = conv(rv, D_FF + c * fc)
        acts[c] = (gate * uv).astype(BF16)
        if c >= 1:
            acc = down(c - 1)
    acc = down(n_ff - 1)
    out = x + mod_ref[0, 5:6, :] * acc
    if final:
        out = out * lax.rsqrt(jnp.mean(out * out, axis=-1, keepdims=True) + NORM_EPS) * fg_ref[...]
    o_ref[0] = out


def _ffn(x, mod, gain, w_up, conv_w, conv_b, w_down, final_g, *, final, tm=512, fc=256):
    B, T, D = x.shape
    row3 = lambda b, j: (b, j, 0)
    return pl.pallas_call(
        functools.partial(_ffn_body, tm=tm, fc=fc, final=final),
        out_shape=jax.ShapeDtypeStruct((B, T, D), F32),
        grid=(B, T // tm),
        in_specs=[pl.BlockSpec((1, tm, D), row3),
                  pl.BlockSpec((1, 6, D), lambda b, j: (b, 0, 0)),
                  _const_spec(gain.shape), _const_spec(w_up.shape), _const_spec(conv_w.shape),
                  _const_spec(conv_b.shape), _const_spec(w_down.shape), _const_spec(final_g.shape)],
        out_specs=pl.BlockSpec((1, tm, D), row3),
        scratch_shapes=[pltpu.VMEM((8, 2 * D_FF), F32)],
        compiler_params=_cparams(("arbitrary", "arbitrary")),
        name="ffn",
    )(x, mod, gain, w_up, conv_w, conv_b, w_down, final_g)


def _cat_weight(w_in_i, vres_down_i):
    D = w_in_i.shape[0]
    o_wl, o_k, o_v, o_al, o_gl, rw_cols = 512, 576, 1088, 1600, 1664, 1792
    o_fox = rw_cols
    o_ff = o_fox + 3 * WIDTH
    o_mq = o_ff + N_HEADS
    o_kr = o_mq + 2 * MLA_LORA
    o_gate = o_kr + MLA_ROPE
    vres = jnp.zeros((D, RW_V_LORA), w_in_i.dtype) if vres_down_i is None else vres_down_i
    pad = jnp.zeros((D, SMALL_W - (S_FF + N_HEADS)), w_in_i.dtype)
    cols = [w_in_i[:, 0:512], w_in_i[:, o_k:o_k + 512], w_in_i[:, o_v:o_v + 512],
            w_in_i[:, o_wl:o_wl + 64], w_in_i[:, o_al:o_al + 64], w_in_i[:, o_gl:o_gl + 128],
            vres, w_in_i[:, o_kr:o_kr + MLA_ROPE], w_in_i[:, o_ff:o_ff + N_HEADS], pad,
            w_in_i[:, o_fox:o_fox + 2 * WIDTH],
            w_in_i[:, o_mq:o_mq + 2 * MLA_LORA],
            w_in_i[:, o_gate:o_gate + N_BRANCH * D_MODEL]]
    w_fv_t = w_in_i[:, o_fox + 2 * WIDTH:o_fox + 3 * WIDTH].T
    return jnp.concatenate(cols, axis=1).astype(BF16), w_fv_t.astype(BF16)


def _cat_mu(mu_i, vres_mu_i):
    o_wl, o_k, o_v, o_al, o_gl = 512, 576, 1088, 1600, 1664
    vres = jnp.zeros((RW_V_LORA,), F32) if vres_mu_i is None else vres_mu_i
    tail = jnp.zeros((SMALL_W - S_KR,), F32)
    return jnp.concatenate([mu_i[0:512], mu_i[o_k:o_k + 512], mu_i[o_v:o_v + 512],
                            mu_i[o_wl:o_wl + 64], mu_i[o_al:o_al + 64], mu_i[o_gl:o_gl + 128],
                            vres, tail]).reshape(1, N_SHIFT)


def _mla_weights(w_q_up, w_kv_up):
    half = MLA_ROPE // 2
    wq = w_q_up.reshape(MLA_LORA, N_HEADS, MLA_QK)
    wq = jnp.concatenate([wq[:, :, :MLA_NOPE].reshape(MLA_LORA, -1),
                          wq[:, :, MLA_NOPE:MLA_NOPE + half].reshape(MLA_LORA, -1),
                          wq[:, :, MLA_NOPE + half:].reshape(MLA_LORA, -1)], axis=1)
    wkv = w_kv_up.reshape(MLA_LORA, N_HEADS, MLA_NOPE + HEAD_DIM)
    wk = wkv[:, :, :MLA_NOPE].reshape(MLA_LORA, -1)
    wv_t = wkv[:, :, MLA_NOPE:].reshape(MLA_LORA, -1).T
    return wq.astype(BF16), wk.astype(BF16), wv_t.astype(BF16)


def kernel(x, c, positions, norm1_g, norm2_g, w_ada, b_ada, w_in, mu_shift, rw_w_up, rw_w0, rw_a_up, rw_a0, rw_g_up, rw_k_k, rw_k_a, rw_r_k, rw_gn_w, rw_gn_b, rw_vres_down, rw_vres_mu, rw_vres_up, rw_v0, fox_b_f, mla_q_norm_g, mla_w_q_up, mla_kv_norm_g, mla_w_kv_up, w_branch, w_o, ffn_w_up, ffn_conv_w, ffn_conv_b, ffn_w_down, final_g):
    B, T, D = x.shape
    depth = w_in.shape[0]
    half = MLA_ROPE // 2
    inv_freq = np.power(np.float32(ROPE_BASE), -np.arange(half, dtype=np.float32) / np.float32(half))
    freq = jnp.asarray(np.tile(inv_freq.astype(np.float32), 128 // half).reshape(1, 128))
    hid = np.arange(WIDTH // 2) // HEAD_DIM
    ones_bd = jnp.asarray((hid[:, None] == hid[None, :]).astype(np.float32), BF16)
    pos3 = positions.reshape(B, T, 1)

    mods = _ada(c, w_ada, b_ada).reshape(depth, B, 6, D)
    rkv0 = None
    for i in range(depth):
        mod = mods[i]
        w_cat, w_fv_t = _cat_weight(w_in[i], None if i == 0 else rw_vres_down[i - 1])
        mu = _cat_mu(mu_shift[i], None if i == 0 else rw_vres_mu[i - 1])
        rkv, small, fq, fk, fvt, lat, gates = _proj(
            x, mod, norm1_g[i].reshape(1, D), w_cat, w_fv_t, mu, fox_b_f[i].reshape(1, N_HEADS))

        zero = jnp.zeros((WIDTH,), F32)
        vecs = jnp.stack([rw_w0[i], rw_a0[i], rw_k_k[i], rw_k_a[i], rw_r_k[i], rw_gn_w[i], rw_gn_b[i],
                          zero if i == 0 else rw_v0[i - 1]])
        y_rw = _rwkv(rkv, small, None if i == 0 else rkv0,
                     rw_w_up[i].astype(BF16), rw_a_up[i].astype(BF16), rw_g_up[i].astype(BF16),
                     None if i == 0 else rw_vres_up[i - 1].astype(BF16), vecs, ones_bd)
        if i == 0:
            rkv0 = rkv

        y_fox = _attn(fq, fk, fvt, name="attn_fox")

        wq, wk, wv_t = _mla_weights(mla_w_q_up[i], mla_w_kv_up[i])
        mq, mk, mvt = _mla_prep(lat, small, pos3, mla_q_norm_g[i].reshape(1, -1),
                                mla_kv_norm_g[i].reshape(1, -1), wq, wk, wv_t, freq)
        y_mla = _attn(mq, mk, mvt, name="attn_mla")

        x = _merge(y_rw, y_fox, y_mla, gates, x, mod,
                   w_branch[i].astype(BF16), w_o[i].astype(BF16))
        x = _ffn(x, mod, norm2_g[i].reshape(1, D), ffn_w_up[i].astype(BF16), ffn_conv_w[i],
                 ffn_conv_b[i].reshape(1, -1), ffn_w_down[i].astype(BF16), final_g.reshape(1, D),
                 final=(i == depth - 1))
    return x
```

```python
import functools

import numpy as np
import jax
import jax.numpy as jnp
from jax import lax
from jax.experimental import pallas as pl
from jax.experimental.pallas import tpu as pltpu

F32 = jnp.float32
BF16 = jnp.bfloat16
HIGHEST = lax.Precision.HIGHEST

D_MODEL = 1024
HEAD_DIM = 64
N_HEADS = 8
WIDTH = N_HEADS * HEAD_DIM
RW_DECAY_LORA = 64
RW_A_LORA = 64
RW_V_LORA = 32
RW_G_LORA = 128
RW_GN_EPS = 64e-5
MLA_NOPE = 64
MLA_ROPE = 32
MLA_QK = MLA_NOPE + MLA_ROPE
MLA_LORA = 256
ROPE_BASE = 10000.0
N_BRANCH = 3
D_FF = 2816
NORM_EPS = 1e-6
NEG_BIG = -1e30

VMEM_LIMIT_BYTES = 56 * 1024 * 1024

C_RKV = 0
C_SMALL = 3 * WIDTH
SMALL_W = 384
S_WL, S_AL, S_GL, S_VR, S_KR, S_FF = 0, 64, 128, 256, 288, 320
C_FOX = C_SMALL + SMALL_W
C_LAT = C_FOX + 2 * WIDTH
C_GATE = C_LAT + 2 * MLA_LORA
N_COLS = C_GATE + N_BRANCH * D_MODEL
N_SHIFT = C_FOX

LOG2E = 1.4426950408889634
FOX_EXTRA = 8
FOX_QK = HEAD_DIM + FOX_EXTRA
VT_ROWS = HEAD_DIM + 16


def _ones_rows(n):
    r = lax.broadcasted_iota(jnp.int32, (VT_ROWS - HEAD_DIM, n), 0)
    return jnp.where(r == 0, 1.0, 0.0).astype(BF16)

RW_CHUNK = 64
RW_GROUP = 2
GW = RW_GROUP * HEAD_DIM


def _cparams(sem):
    return pltpu.CompilerParams(dimension_semantics=sem, vmem_limit_bytes=VMEM_LIMIT_BYTES)


def _const_spec(shape):
    n = len(shape)
    return pl.BlockSpec(shape, lambda *_: (0,) * n, pipeline_mode=pl.Buffered(1))


def _sigmoid(z):
    return 1.0 / (1.0 + jnp.exp(-z))


def _softplus(z):
    return jnp.maximum(z, 0.0) + jnp.log1p(jnp.exp(-jnp.abs(z)))


def _modulated_norm(x, gain, shift, scale):
    ms = jnp.mean(x * x, axis=-1, keepdims=True)
    return (x * lax.rsqrt(ms + NORM_EPS) * gain) * (1.0 + scale) + shift


def _ada_body(c_ref, w_ref, b_ref, o_ref):
    c = c_ref[...]
    ca = c * _sigmoid(c)
    o_ref[0] = jnp.dot(ca, w_ref[0], precision=HIGHEST, preferred_element_type=F32) + b_ref[0]


def _ada(c, w_ada, b_ada):
    L, D, N6 = w_ada.shape
    B = c.shape[0]
    tn = 1536
    return pl.pallas_call(
        _ada_body,
        out_shape=jax.ShapeDtypeStruct((L, B, N6), F32),
        grid=(L, N6 // tn),
        in_specs=[pl.BlockSpec((B, D), lambda l, n: (0, 0)),
                  pl.BlockSpec((1, D, tn), lambda l, n: (l, 0, n)),
                  pl.BlockSpec((1, 1, tn), lambda l, n: (l, 0, n))],
        out_specs=pl.BlockSpec((1, B, tn), lambda l, n: (l, 0, n)),
        compiler_params=_cparams(("arbitrary", "arbitrary")),
        name="ada",
    )(c, w_ada, b_ada.reshape(L, 1, N6))


def _bf16_round(t):
    return t.astype(BF16).astype(F32)


def _mla_heads(lat, kr, pos, gq_ref, gkv_ref, wq_ref, wk_ref, wvt_ref, freq_ref, q_ref, k_ref, vt_ref):
    def rms(t, gain):
        return t * lax.rsqrt(jnp.mean(t * t, axis=-1, keepdims=True) + NORM_EPS) * gain

    qn = rms(lat[:, :MLA_LORA], gq_ref[...]).astype(BF16)
    kvn = rms(lat[:, MLA_LORA:], gkv_ref[...]).astype(BF16)
    q = jnp.dot(qn, wq_ref[...], preferred_element_type=F32)
    kn = jnp.dot(kvn, wk_ref[...], preferred_element_type=F32)
    vt = lax.dot_general(wvt_ref[...], kvn, (((1,), (1,)), ((), ())), preferred_element_type=F32)

    ang = pos.astype(F32) * freq_ref[...]
    cos, sin = jnp.cos(ang), jnp.sin(ang)
    half = MLA_ROPE // 2
    q1, q2 = q[:, WIDTH:WIDTH + 128], q[:, WIDTH + 128:WIDTH + 256]
    qo1 = q1 * cos - q2 * sin
    qo2 = q1 * sin + q2 * cos
    k1, k2 = kr[:, :half], kr[:, half:]
    c16, s16 = cos[:, :half], sin[:, :half]
    ko = jnp.concatenate([k1 * c16 - k2 * s16, k1 * s16 + k2 * c16], axis=-1)

    scale = MLA_QK ** -0.5 * LOG2E
    for hd in range(N_HEADS):
        hs = slice(hd * HEAD_DIM, (hd + 1) * HEAD_DIM)
        ps = slice(hd * half, (hd + 1) * half)
        qh = jnp.concatenate([q[:, hs], qo1[:, ps], qo2[:, ps]], axis=-1) * scale
        q_ref[0, hd] = qh.astype(BF16)
        k_ref[0, hd] = jnp.concatenate([kn[:, hs], ko], axis=-1).astype(BF16)
        vt_ref[0, hd, 0:HEAD_DIM, :] = vt[hs, :].astype(BF16)
        vt_ref[0, hd, HEAD_DIM:VT_ROWS, :] = _ones_rows(vt.shape[1])


def _proj_body(x_ref, mod_ref, g_ref, w_ref, wvt_ref, mu_ref, bf_ref,
               pos_ref, gq_ref, gkv_ref, wq_ref, wk_ref, mwvt_ref, freq_ref,
               rkv_ref, small_ref, fq_ref, fk_ref, fvt_ref, gate_ref, mq_ref, mk_ref, mvt_ref,
               carry_ref, cc_ref, *, tm):
    j = pl.program_id(1)

    @pl.when(j == 0)
    def _():
        carry_ref[...] = jnp.zeros_like(carry_ref)
        cc_ref[...] = jnp.zeros_like(cc_ref)

    h = _modulated_norm(x_ref[0], g_ref[...], mod_ref[0, 0:1, :], mod_ref[0, 1:2, :])
    hb = h.astype(BF16)
    row = lax.broadcasted_iota(jnp.int32, (tm, 1), 0)

    def mm(c0, c1):
        return jnp.dot(hb, w_ref[:, c0:c1], preferred_element_type=F32)

    def shifted(c0, c1):
        res = mm(c0, c1)
        last = carry_ref[0:1, c0:c1]
        carry_ref[0:1, c0:c1] = res[tm - 1:tm, :]
        prev = jnp.where(row == 0, last, pltpu.roll(res, 1, 0))
        return res + (prev - res) * mu_ref[:, c0:c1]

    for i in range(3):
        c0 = C_RKV + i * WIDTH
        rkv_ref[0, :, c0:c0 + WIDTH] = shifted(c0, c0 + WIDTH).astype(BF16)
    small = shifted(C_SMALL, C_SMALL + SMALL_W)
    small_ref[0] = small

    z = small[:, S_FF:S_FF + N_HEADS] + bf_ref[...]
    logf = jnp.minimum(z, 0.0) - jnp.log1p(jnp.exp(-jnp.abs(z)))
    ri = lax.broadcasted_iota(jnp.int32, (tm, tm), 0)
    ci = lax.broadcasted_iota(jnp.int32, (tm, tm), 1)
    cum = jnp.dot((ci <= ri).astype(F32), logf, precision=HIGHEST,
                  preferred_element_type=F32) + cc_ref[0:1, 0:N_HEADS]
    cc_ref[0:1, 0:N_HEADS] = cum[tm - 1:tm, :]
    cum2 = cum * LOG2E

    fq = mm(C_FOX, C_FOX + WIDTH) * (HEAD_DIM ** -0.5 * LOG2E)
    fk = mm(C_FOX + WIDTH, C_FOX + 2 * WIDTH)
    lane = lax.broadcasted_iota(jnp.int32, (tm, FOX_EXTRA), 1)
    for hd in range(N_HEADS):
        c8 = jnp.broadcast_to(cum2[:, hd:hd + 1], (tm, FOX_EXTRA))
        hi = _bf16_round(c8)
        mid = _bf16_round(c8 - hi)
        lo = _bf16_round(c8 - hi - mid)
        eq = jnp.where(lane == 0, hi, jnp.where(lane == 1, mid, jnp.where(lane == 2, lo,
                       jnp.where(lane < 6, 1.0, 0.0))))
        ek = jnp.where(lane < 3, 1.0, jnp.where(lane == 3, -hi, jnp.where(lane == 4, -mid,
                       jnp.where(lane == 5, -lo, 0.0))))
        hs = slice(hd * HEAD_DIM, (hd + 1) * HEAD_DIM)
        fq_ref[0, hd] = jnp.concatenate([fq[:, hs], eq], axis=-1).astype(BF16)
        fk_ref[0, hd] = jnp.concatenate([fk[:, hs], ek], axis=-1).astype(BF16)

    fvt = lax.dot_general(wvt_ref[...], hb, (((1,), (1,)), ((), ())), preferred_element_type=F32)
    for hd in range(N_HEADS):
        fvt_ref[0, hd, 0:HEAD_DIM, :] = fvt[hd * HEAD_DIM:(hd + 1) * HEAD_DIM, :].astype(BF16)
        fvt_ref[0, hd, HEAD_DIM:VT_ROWS, :] = _ones_rows(tm)

    lat = mm(C_LAT, C_GATE)
    n_gate = N_BRANCH * D_MODEL // WIDTH

    def gate_chunk(i):
        c0 = C_GATE + i * WIDTH
        gate_ref[0, :, i * WIDTH:(i + 1) * WIDTH] = _sigmoid(mm(c0, c0 + WIDTH)).astype(BF16)

    for i in range(n_gate // 2):
        gate_chunk(i)
    _mla_heads(lat, small[:, S_KR:S_KR + MLA_ROPE], pos_ref[0], gq_ref, gkv_ref, wq_ref, wk_ref, mwvt_ref,
               freq_ref, mq_ref, mk_ref, mvt_ref)
    for i in range(n_gate // 2, n_gate):
        gate_chunk(i)


def _proj(x, mod, gain, w_cat, w_fv_t, mu, b_f, positions, gq, gkv, wq, wk, wv_t, freq, *, tm=256):
    B, T, D = x.shape
    nt = T // tm
    row3 = lambda b, j: (b, j, 0)
    hm = lambda d: (jax.ShapeDtypeStruct((B, N_HEADS, T, d), BF16),
                    pl.BlockSpec((1, N_HEADS, tm, d), lambda b, j: (b, 0, j, 0)))
    vt = (jax.ShapeDtypeStruct((B, N_HEADS, VT_ROWS, T), BF16),
          pl.BlockSpec((1, N_HEADS, VT_ROWS, tm), lambda b, j: (b, 0, 0, j)))
    outs = [(jax.ShapeDtypeStruct((B, T, 3 * WIDTH), BF16), pl.BlockSpec((1, tm, 3 * WIDTH), row3)),
            (jax.ShapeDtypeStruct((B, T, SMALL_W), F32), pl.BlockSpec((1, tm, SMALL_W), row3)),
            hm(FOX_QK), hm(FOX_QK), vt,
            (jax.ShapeDtypeStruct((B, T, N_BRANCH * D), BF16),
             pl.BlockSpec((1, tm, N_BRANCH * D), row3)),
            hm(MLA_QK), hm(MLA_QK), vt]
    consts = [gain, w_cat, w_fv_t, mu, b_f]
    mla_consts = [gq, gkv, wq, wk, wv_t, freq]
    return pl.pallas_call(
        functools.partial(_proj_body, tm=tm),
        out_shape=tuple(o[0] for o in outs),
        grid=(B, nt),
        in_specs=([pl.BlockSpec((1, tm, D), row3), pl.BlockSpec((1, 6, D), lambda b, j: (b, 0, 0))]
                  + [_const_spec(t.shape) for t in consts]
                  + [pl.BlockSpec((1, tm, 1), row3)]
                  + [_const_spec(t.shape) for t in mla_consts]),
        out_specs=tuple(o[1] for o in outs),
        scratch_shapes=[pltpu.VMEM((8, N_SHIFT), F32),
                        pltpu.VMEM((8, 128), F32)],
        compiler_params=_cparams(("arbitrary", "arbitrary")),
        name="proj",
    )(x, mod, *consts, positions, *mla_consts)


def _rwkv_body(*refs, has_vres, n_chunks):
    if has_vres:
        (rkv_ref, small_ref, vfirst_ref, lw_ref, la_ref, lg_ref, lv_ref, vec_ref, ones_ref,
         y_ref, ht_ref) = refs
    else:
        rkv_ref, small_ref, lw_ref, la_ref, lg_ref, vec_ref, ones_ref, y_ref, ht_ref = refs
    C = RW_CHUNK
    TC = n_chunks * C
    j = pl.program_id(1)

    @pl.when(j == 0)
    def _():
        ht_ref[...] = jnp.zeros_like(ht_ref)

    def vec(i):
        return vec_ref[i:i + 1, :]

    w0, a0, k_k, k_a, r_k, gn_w, gn_b, v0 = (vec(i) for i in range(8))
    ones_bd = ones_ref[...]
    HW = ones_bd.shape[0]

    def headsum(t):
        tb = t.astype(BF16)
        return jnp.concatenate([jnp.dot(tb[:, c0:c0 + HW], ones_bd, preferred_element_type=F32)
                                for c0 in range(0, WIDTH, HW)], axis=1)

    def lora(t, w_ref):
        return jnp.dot(t.astype(BF16), w_ref[...], preferred_element_type=F32)

    r = rkv_ref[0, :, 0:WIDTH].astype(F32)
    k = rkv_ref[0, :, WIDTH:2 * WIDTH].astype(F32)
    v = rkv_ref[0, :, 2 * WIDTH:3 * WIDTH].astype(F32)
    sm = small_ref[0]
    wl = sm[:, S_WL:S_WL + RW_DECAY_LORA]
    al = sm[:, S_AL:S_AL + RW_A_LORA]
    gl = sm[:, S_GL:S_GL + RW_G_LORA]

    w = -_softplus(-(w0 + lora(jnp.tanh(wl), lw_ref))) - 0.5
    logd = -jnp.exp(w)
    a = _sigmoid(a0 + lora(al, la_ref))
    g = lora(_sigmoid(gl), lg_ref)
    if has_vres:
        lvv = sm[:, S_VR:S_VR + RW_V_LORA]
        v = v + (vfirst_ref[0].astype(F32) - v) * _sigmoid(v0 + lora(lvv, lv_ref))
    kk = k * k_k
    kk = kk * lax.rsqrt(jnp.maximum(headsum(kk * kk), 1e-24))
    k2 = k * (1.0 + (a - 1.0) * k_a)

    ri = lax.broadcasted_iota(jnp.int32, (TC, TC), 0)
    ci = lax.broadcasted_iota(jnp.int32, (TC, TC), 1)
    tri = jnp.where((ci <= ri) & ((ri // C) == (ci // C)), 1.0, 0.0).astype(BF16)
    ld_hi = logd.astype(BF16)
    ld_mid = (logd - ld_hi.astype(F32)).astype(BF16)
    ld_lo = (logd - ld_hi.astype(F32) - ld_mid.astype(F32)).astype(BF16)
    cl = (jnp.dot(tri, ld_hi, preferred_element_type=F32) + jnp.dot(tri, ld_mid, preferred_element_type=F32)
          + jnp.dot(tri, ld_lo, preferred_element_type=F32))
    cl_last = jnp.concatenate([jnp.broadcast_to(cl[(c + 1) * C - 1:(c + 1) * C, :], (C, WIDTH))
                               for c in range(n_chunks)], axis=0)
    e_neg = jnp.exp(-cl)
    e_end = jnp.exp(cl_last - cl)
    ka = kk * a
    a_t = kk * jnp.exp(cl - logd)
    r_t = r * jnp.exp(cl)
    b_m = -ka * e_neg
    k_m = k2 * e_neg
    b_p = -ka * e_end
    k_p = k2 * e_end
    p_c = jnp.exp(cl_last)

    bi = lax.broadcasted_iota(jnp.int32, (GW, GW), 0)
    bj = lax.broadcasted_iota(jnp.int32, (GW, GW), 1)
    bd_mask = (bi // HEAD_DIM) == (bj // HEAD_DIM)
    strict = bi > bj
    incl = bi >= bj
    eye = (bi == bj).astype(F32)

    def bd(t):
        return jnp.where(bd_mask, jnp.concatenate([t] * RW_GROUP, axis=0), 0.0)

    def collapse(t):
        return sum(t[hh * C:(hh + 1) * C, :] for hh in range(RW_GROUP))

    def dot16(p, q):
        return jnp.dot(p.astype(BF16), q.astype(BF16), preferred_element_type=F32)

    chains = [(ch, gi) for ch in range(n_chunks) for gi in range(N_HEADS // RW_GROUP)]
    nc = len(chains)

    def piece(t, c):
        ch, gi = chains[c]
        return t[ch * C:(ch + 1) * C, gi * GW:(gi + 1) * GW]

    def bds(t):
        return [bd(piece(t, c)).astype(BF16) for c in range(nc)]

    at_bd, rt_bd, bm_bd, km_bd, bp_bd, kp_bd, v_bd = (bds(t) for t in (a_t, r_t, b_m, k_m, b_p, k_p, v))
    a_all = [lax.dot_general(jnp.concatenate([at_bd[c], rt_bd[c]], axis=0),
                             jnp.concatenate([bm_bd[c], km_bd[c]], axis=0),
                             (((1,), (1,)), ((), ())), preferred_element_type=F32) for c in range(nc)]
    a_ab = [jnp.where(strict, a_all[c][:GW, :GW], 0.0) for c in range(nc)]
    a_ak = [jnp.where(strict, a_all[c][:GW, GW:], 0.0).astype(BF16) for c in range(nc)]
    m_rb = [jnp.where(incl, a_all[c][GW:, :GW], 0.0).astype(BF16) for c in range(nc)]
    m_rk = [jnp.where(incl, a_all[c][GW:, GW:], 0.0).astype(BF16) for c in range(nc)]

    pw = [t.astype(BF16) for t in a_ab]
    tinv = [eye + t for t in a_ab]
    n_sq = int(np.log2(C)) - 1
    for it in range(n_sq):
        pw = [jnp.dot(pw[c], pw[c], preferred_element_type=F32).astype(BF16) for c in range(nc)]
        tinv = [tinv[c] + dot16(tinv[c], pw[c]) for c in range(nc)]
    tinv = [t.astype(BF16) for t in tinv]

    akv = [dot16(a_ak[c], v_bd[c]) for c in range(nc)]
    x = [dot16(tinv[c], jnp.concatenate([at_bd[c], akv[c].astype(BF16)], axis=1)) for c in range(nc)]
    wm_bd = [t[:, :GW].astype(BF16) for t in x]
    u0_bd = [t[:, GW:].astype(BF16) for t in x]
    y1_bd = [dot16(jnp.concatenate([m_rk[c], m_rb[c]], axis=1),
                   jnp.concatenate([v_bd[c], u0_bd[c]], axis=0)) for c in range(nc)]
    rm_bd = [dot16(m_rb[c], wm_bd[c]) for c in range(nc)]
    qt = [dot16(x[c][:, :GW].T, bp_bd[c]) for c in range(nc)]
    h1t = [dot16(jnp.concatenate([x[c][:, GW:], v_bd[c].astype(F32)], axis=0).T,
                 jnp.concatenate([bp_bd[c], kp_bd[c]], axis=0)) for c in range(nc)]
    local = {}
    for c, (ch, gi) in enumerate(chains):
        local[ch, gi] = (collapse(y1_bd[c]), (piece(r_t, c) + collapse(rm_bd[c])).astype(BF16),
                         qt[c].astype(BF16), h1t[c], p_c[ch * C:ch * C + 1, gi * GW:(gi + 1) * GW])

    y_rows = []
    for ch in range(n_chunks):
        ys = []
        for gi in range(N_HEADS // RW_GROUP):
            y1, rm, qt, h1t, pc = local[ch, gi]
            ht = ht_ref[gi]
            htb = ht.astype(BF16)
            ys.append(y1 + lax.dot_general(rm, htb, (((1,), (1,)), ((), ())), preferred_element_type=F32))
            ht_ref[gi] = ht * pc + jnp.dot(htb, qt, preferred_element_type=F32) + h1t
        y_rows.append(jnp.concatenate(ys, axis=1))
    y = jnp.concatenate(y_rows, axis=0)
    inv_n = 1.0 / HEAD_DIM
    mean = headsum(y) * inv_n
    dlt = y - mean
    var = headsum(dlt * dlt) * inv_n
    y = dlt * lax.rsqrt(var + RW_GN_EPS) * gn_w + gn_b
    y = y + headsum(r * k2 * r_k) * v
    y_ref[0] = (y * g).astype(BF16)


def _rwkv(rkv, small, vfirst_rkv, lw, la, lg, lv, vecs, ones_bd, *, n_chunks=4):
    B, T, _ = rkv.shape
    C = n_chunks * RW_CHUNK
    has_vres = vfirst_rkv is not None
    row3 = lambda b, j: (b, j, 0)
    in_specs = [pl.BlockSpec((1, C, 3 * WIDTH), row3), pl.BlockSpec((1, C, SMALL_W), row3)]
    args = [rkv, small]
    if has_vres:
        in_specs.append(pl.BlockSpec((1, C, WIDTH), lambda b, j: (b, j, 2)))
        args.append(vfirst_rkv)
    w_list = [lw, la, lg] + ([lv] if has_vres else []) + [vecs, ones_bd]
    in_specs += [_const_spec(t.shape) for t in w_list]
    args += w_list
    return pl.pallas_call(
        functools.partial(_rwkv_body, has_vres=has_vres, n_chunks=n_chunks),
        out_shape=jax.ShapeDtypeStruct((B, T, WIDTH), BF16),
        grid=(B, T // C),
        in_specs=in_specs,
        out_specs=pl.BlockSpec((1, C, WIDTH), row3),
        scratch_shapes=[pltpu.VMEM((N_HEADS // RW_GROUP, GW, GW), F32)],
        compiler_params=_cparams(("arbitrary", "arbitrary")),
        name="rwkv",
    )(*args)


def _attn_body(q_ref, k_ref, vt_ref, o_ref, m_ref, acc_ref, *, tq, tks):
    qi = pl.program_id(2)
    tk = tq
    hps = q_ref.shape[1]

    m_ref[...] = jnp.full_like(m_ref, NEG_BIG)
    acc_ref[...] = jnp.zeros_like(acc_ref)

    def step(ki, diag):
        k0 = pl.multiple_of(ki * tk, tk)
        nsub = tk // tks
        units = [(i, s) for s in range(nsub) for i in range(hps)]

        def lo_of(s):
            return s * tks if diag else 0

        def qk(i, s):
            kb = k_ref[0, i, pl.ds(k0 + s * tks, tks), :]
            st = lax.dot_general(kb, q_ref[0, i, lo_of(s):, :], (((1,), (1,)), ((), ())),
                                 preferred_element_type=F32)
            if diag:
                key = lax.broadcasted_iota(jnp.int32, st.shape, 0)
                qry = lax.broadcasted_iota(jnp.int32, st.shape, 1)
                st = jnp.where(key <= qry, st, NEG_BIG)
            return st

        def softmax(i, s, st):
            lo = lo_of(s)
            m_old = m_ref[i, :, lo:]
            m_new = jnp.maximum(m_old, jnp.max(st, axis=0, keepdims=True))
            m_ref[i, :, lo:] = m_new
            return jnp.exp2(st - m_new).astype(BF16), jnp.exp2(m_old - m_new)

        def pv(i, s, p, alpha):
            lo = lo_of(s)
            vtb = vt_ref[0, i, :, pl.ds(k0 + s * tks, tks)]
            acc_ref[i, :, lo:] = alpha * acc_ref[i, :, lo:] + jnp.dot(vtb, p, preferred_element_type=F32)

        n = len(units)
        sc = {u: qk(*units[u]) for u in range(min(2, n))}
        pr = {}
        for u in range(n):
            pr[u] = softmax(*units[u], sc.pop(u))
            if u + 2 < n:
                sc[u + 2] = qk(*units[u + 2])
            if u >= 1:
                pv(*units[u - 1], *pr.pop(u - 1))
        pv(*units[n - 1], *pr.pop(n - 1))

    def loop_body(ki, carry):
        step(ki, False)
        return carry

    lax.fori_loop(0, qi, loop_body, 0)
    step(qi, True)

    out_t = jnp.concatenate([acc_ref[i, 0:HEAD_DIM, :] / acc_ref[i, HEAD_DIM:HEAD_DIM + 1, :]
                             for i in range(hps)], axis=0)
    o_ref[0] = out_t.T.astype(BF16)


def _attn(q, k, vt, *, name, tq=512, tks=256, hps=8):
    B, H, T, dk = q.shape
    nq = T // tq
    return pl.pallas_call(
        functools.partial(_attn_body, tq=tq, tks=tks),
        out_shape=jax.ShapeDtypeStruct((B, T, H * HEAD_DIM), BF16),
        grid=(B, H // hps, nq),
        in_specs=[pl.BlockSpec((1, hps, tq, dk), lambda b, p, i: (b, p, i, 0)),
                  pl.BlockSpec((1, hps, T, dk), lambda b, p, i: (b, p, 0, 0)),
                  pl.BlockSpec((1, hps, VT_ROWS, T), lambda b, p, i: (b, p, 0, 0))],
        out_specs=pl.BlockSpec((1, tq, hps * HEAD_DIM), lambda b, p, i: (b, i, p)),
        scratch_shapes=[pltpu.VMEM((hps, 1, tq), F32), pltpu.VMEM((hps, VT_ROWS, tq), F32)],
        compiler_params=_cparams(("arbitrary", "arbitrary", "arbitrary")),
        name=name,
    )(q, k, vt)


def _merge_body(yr_ref, yf_ref, ym_ref, gate_ref, x_ref, mod_ref, wb_ref, wo_ref, o_ref):
    D = D_MODEL
    mixed = None
    for gi, y_ref in enumerate((yr_ref, yf_ref, ym_ref)):
        bd = jnp.dot(y_ref[0], wb_ref[gi], preferred_element_type=F32)
        term = gate_ref[0, :, gi * D:(gi + 1) * D].astype(F32) * bd
        mixed = term if mixed is None else mixed + term
    out = jnp.dot(mixed.astype(BF16), wo_ref[...], preferred_element_type=F32)
    o_ref[0] = x_ref[0] + mod_ref[0, 2:3, :] * out


def _merge(y_rw, y_fox, y_mla, gates, x, mod, wb, wo, *, tm=1024):
    B, T, D = x.shape
    row3 = lambda b, j: (b, j, 0)
    yspec = pl.BlockSpec((1, tm, WIDTH), row3)
    return pl.pallas_call(
        _merge_body,
        out_shape=jax.ShapeDtypeStruct((B, T, D), F32),
        grid=(B, T // tm),
        in_specs=[yspec, yspec, yspec,
                  pl.BlockSpec((1, tm, N_BRANCH * D), row3),
                  pl.BlockSpec((1, tm, D), row3),
                  pl.BlockSpec((1, 6, D), lambda b, j: (b, 0, 0)),
                  _const_spec(wb.shape), _const_spec(wo.shape)],
        out_specs=pl.BlockSpec((1, tm, D), row3),
        compiler_params=_cparams(("arbitrary", "arbitrary")),
        name="merge",
    )(y_rw, y_fox, y_mla, gates, x, mod, wb, wo)


def _ffn_body(x_ref, mod_ref, g_ref, wup_ref, cw_ref, cb_ref, wdn_ref, fg_ref, o_ref, carry_ref,
              *, tm, fc, final):
    j = pl.program_id(1)

    @pl.when(j == 0)
    def _():
        carry_ref[...] = jnp.zeros_like(carry_ref)

    x = x_ref[0]
    hb = _modulated_norm(x, g_ref[...], mod_ref[0, 3:4, :], mod_ref[0, 4:5, :]).astype(BF16)
    row8 = lax.broadcasted_iota(jnp.int32, (8, 1), 0)

    def up(c0):
        return jnp.dot(hb, wup_ref[:, c0:c0 + fc], preferred_element_type=F32)

    def conv(u, c0):
        t2 = carry_ref[0:1, c0:c0 + fc]
        t1 = carry_ref[1:2, c0:c0 + fc]
        carry_ref[0:2, c0:c0 + fc] = u[tm - 2:tm, :]
        r1 = pltpu.roll(u, 1, 0)
        r2 = pltpu.roll(u, 2, 0)
        p1 = jnp.concatenate([jnp.where(row8 == 0, t1, r1[:8]), r1[8:]], axis=0)
        p2 = jnp.concatenate([jnp.where(row8 == 0, t2, jnp.where(row8 == 1, t1, r2[:8])), r2[8:]], axis=0)
        return (u * cw_ref[2:3, c0:c0 + fc] + p1 * cw_ref[1:2, c0:c0 + fc]
                + p2 * cw_ref[0:1, c0:c0 + fc] + cb_ref[:, c0:c0 + fc])

    n_ff = D_FF // fc
    raw = {0: (up(0), up(D_FF))}
    acts = {}
    acc = None

    def down(c):
        part = jnp.dot(acts.pop(c), wdn_ref[c * fc:(c + 1) * fc, :], preferred_element_type=F32)
        return part if acc is None else acc + part

    for c in range(n_ff):
        rg, rv = raw.pop(c)
        if c + 1 < n_ff:
            nxt_g = up((c + 1) * fc)
        ug = conv(rg, c * fc)
        gate = ug * _sigmoid(ug)
        if c + 1 < n_ff:
            raw[c + 1] = (nxt_g, up(D_FF + (c + 1) * fc))
        uv = conv(rv, D_FF + c * fc)
        acts[c] = (gate * uv).astype(BF16)
        if c >= 1:
            acc = down(c - 1)
    acc = down(n_ff - 1)
    out = x + mod_ref[0, 5:6, :] * acc
    if final:
        out = out * lax.rsqrt(jnp.mean(out * out, axis=-1, keepdims=True) + NORM_EPS) * fg_ref[...]
    o_ref[0] = out


def _ffn(x, mod, gain, w_up, conv_w, conv_b, w_down, final_g, *, final, tm=512, fc=256):
    B, T, D = x.shape
    row3 = lambda b, j: (b, j, 0)
    return pl.pallas_call(
        functools.partial(_ffn_body, tm=tm, fc=fc, final=final),
        out_shape=jax.ShapeDtypeStruct((B, T, D), F32),
        grid=(B, T // tm),
        in_specs=[pl.BlockSpec((1, tm, D), row3),
                  pl.BlockSpec((1, 6, D), lambda b, j: (b, 0, 0)),
                  _const_spec(gain.shape), _const_spec(w_up.shape), _const_spec(conv_w.shape),
                  _const_spec(conv_b.shape), _const_spec(w_down.shape), _const_spec(final_g.shape)],
        out_specs=pl.BlockSpec((1, tm, D), row3),
        scratch_shapes=[pltpu.VMEM((8, 2 * D_FF), F32)],
        compiler_params=_cparams(("arbitrary", "arbitrary")),
        name="ffn",
    )(x, mod, gain, w_up, conv_w, conv_b, w_down, final_g)


def _cat_weight(w_in_i, vres_down_i):
    D = w_in_i.shape[0]
    o_wl, o_k, o_v, o_al, o_gl, rw_cols = 512, 576, 1088, 1600, 1664, 1792
    o_fox = rw_cols
    o_ff = o_fox + 3 * WIDTH
    o_mq = o_ff + N_HEADS
    o_kr = o_mq + 2 * MLA_LORA
    o_gate = o_kr + MLA_ROPE
    vres = jnp.zeros((D, RW_V_LORA), w_in_i.dtype) if vres_down_i is None else vres_down_i
    pad = jnp.zeros((D, SMALL_W - (S_FF + N_HEADS)), w_in_i.dtype)
    cols = [w_in_i[:, 0:512], w_in_i[:, o_k:o_k + 512], w_in_i[:, o_v:o_v + 512],
            w_in_i[:, o_wl:o_wl + 64], w_in_i[:, o_al:o_al + 64], w_in_i[:, o_gl:o_gl + 128],
            vres, w_in_i[:, o_kr:o_kr + MLA_ROPE], w_in_i[:, o_ff:o_ff + N_HEADS], pad,
            w_in_i[:, o_fox:o_fox + 2 * WIDTH],
            w_in_i[:, o_mq:o_mq + 2 * MLA_LORA],
            w_in_i[:, o_gate:o_gate + N_BRANCH * D_MODEL]]
    w_fv_t = w_in_i[:, o_fox + 2 * WIDTH:o_fox + 3 * WIDTH].T
    return jnp.concatenate(cols, axis=1).astype(BF16), w_fv_t.astype(BF16)


def _cat_mu(mu_i, vres_mu_i):
    o_wl, o_k, o_v, o_al, o_gl = 512, 576, 1088, 1600, 1664
    vres = jnp.zeros((RW_V_LORA,), F32) if vres_mu_i is None else vres_mu_i
    tail = jnp.zeros((SMALL_W - S_KR,), F32)
    return jnp.concatenate([mu_i[0:512], mu_i[o_k:o_k + 512], mu_i[o_v:o_v + 512],
                            mu_i[o_wl:o_wl + 64], mu_i[o_al:o_al + 64], mu_i[o_gl:o_gl + 128],
                            vres, tail]).reshape(1, N_SHIFT)


def _mla_weights(w_q_up, w_kv_up):
    half = MLA_ROPE // 2
    wq = w_q_up.reshape(MLA_LORA, N_HEADS, MLA_QK)
    wq = jnp.concatenate([wq[:, :, :MLA_NOPE].reshape(MLA_LORA, -1),
                          wq[:, :, MLA_NOPE:MLA_NOPE + half].reshape(MLA_LORA, -1),
                          wq[:, :, MLA_NOPE + half:].reshape(MLA_LORA, -1)], axis=1)
    wkv = w_kv_up.reshape(MLA_LORA, N_HEADS, MLA_NOPE + HEAD_DIM)
    wk = wkv[:, :, :MLA_NOPE].reshape(MLA_LORA, -1)
    wv_t = wkv[:, :, MLA_NOPE:].reshape(MLA_LORA, -1).T
    return wq.astype(BF16), wk.astype(BF16), wv_t.astype(BF16)


def kernel(x, c, positions, norm1_g, norm2_g, w_ada, b_ada, w_in, mu_shift, rw_w_up, rw_w0, rw_a_up, rw_a0, rw_g_up, rw_k_k, rw_k_a, rw_r_k, rw_gn_w, rw_gn_b, rw_vres_down, rw_vres_mu, rw_vres_up, rw_v0, fox_b_f, mla_q_norm_g, mla_w_q_up, mla_kv_norm_g, mla_w_kv_up, w_branch, w_o, ffn_w_up, ffn_conv_w, ffn_conv_b, ffn_w_down, final_g):
    B, T, D = x.shape
    depth = w_in.shape[0]
    half = MLA_ROPE // 2
    inv_freq = np.power(np.float32(ROPE_BASE), -np.arange(half, dtype=np.float32) / np.float32(half))
    freq = jnp.asarray(np.tile(inv_freq.astype(np.float32), 128 // half).reshape(1, 128))
    hid = np.arange(WIDTH // 2) // HEAD_DIM
    ones_bd = jnp.asarray((hid[:, None] == hid[None, :]).astype(np.float32), BF16)
    pos3 = positions.reshape(B, T, 1)

    mods = _ada(c, w_ada, b_ada).reshape(depth, B, 6, D)
    rkv0 = None
    for i in range(depth):
        mod = mods[i]
        w_cat, w_fv_t = _cat_weight(w_in[i], None if i == 0 else rw_vres_down[i - 1])
        mu = _cat_mu(mu_shift[i], None if i == 0 else rw_vres_mu[i - 1])
        wq, wk, wv_t = _mla_weights(mla_w_q_up[i], mla_w_kv_up[i])
        rkv, small, fq, fk, fvt, gates, mq, mk, mvt = _proj(
            x, mod, norm1_g[i].reshape(1, D), w_cat, w_fv_t, mu, fox_b_f[i].reshape(1, N_HEADS),
            pos3, mla_q_norm_g[i].reshape(1, -1), mla_kv_norm_g[i].reshape(1, -1), wq, wk, wv_t, freq)

        zero = jnp.zeros((WIDTH,), F32)
        vecs = jnp.stack([rw_w0[i], rw_a0[i], rw_k_k[i], rw_k_a[i], rw_r_k[i], rw_gn_w[i], rw_gn_b[i],
                          zero if i == 0 else rw_v0[i - 1]])
        y_rw = _rwkv(rkv, small, None if i == 0 else rkv0,
                     rw_w_up[i].astype(BF16), rw_a_up[i].astype(BF16), rw_g_up[i].astype(BF16),
                     None if i == 0 else rw_vres_up[i - 1].astype(BF16), vecs, ones_bd)
        if i == 0:
            rkv0 = rkv

        y_fox = _attn(fq, fk, fvt, name="attn_fox")

        y_mla = _attn(mq, mk, mvt, name="attn_mla")

        x = _merge(y_rw, y_fox, y_mla, gates, x, mod,
                   w_branch[i].astype(BF16), w_o[i].astype(BF16))
        x = _ffn(x, mod, norm2_g[i].reshape(1, D), ffn_w_up[i].astype(BF16), ffn_conv_w[i],
                 ffn_conv_b[i].reshape(1, -1), ffn_w_down[i].astype(BF16), final_g.reshape(1, D),
                 final=(i == depth - 1))
    return x
```

```python
import functools

import numpy as np
import jax
import jax.numpy as jnp
from jax import lax
from jax.experimental import pallas as pl
from jax.experimental.pallas import tpu as pltpu

F32 = jnp.float32
BF16 = jnp.bfloat16
HIGHEST = lax.Precision.HIGHEST

D_MODEL = 1024
HEAD_DIM = 64
N_HEADS = 8
WIDTH = N_HEADS * HEAD_DIM
RW_DECAY_LORA = 64
RW_A_LORA = 64
RW_V_LORA = 32
RW_G_LORA = 128
RW_GN_EPS = 64e-5
MLA_NOPE = 64
MLA_ROPE = 32
MLA_QK = MLA_NOPE + MLA_ROPE
MLA_LORA = 256
ROPE_BASE = 10000.0
N_BRANCH = 3
D_FF = 2816
NORM_EPS = 1e-6
NEG_BIG = -1e30

VMEM_LIMIT_BYTES = 56 * 1024 * 1024

C_RKV = 0
C_SMALL = 3 * WIDTH
SMALL_W = 384
S_WL, S_AL, S_GL, S_VR, S_KR, S_FF = 0, 64, 128, 256, 288, 320
C_FOX = C_SMALL + SMALL_W
C_LAT = C_FOX + 2 * WIDTH
C_GATE = C_LAT + 2 * MLA_LORA
N_COLS = C_GATE + N_BRANCH * D_MODEL
N_SHIFT = C_FOX

LOG2E = 1.4426950408889634
FOX_EXTRA = 8
FOX_QK = HEAD_DIM + FOX_EXTRA
VT_ROWS = HEAD_DIM + 16


def _ones_rows(n):
    r = lax.broadcasted_iota(jnp.int32, (VT_ROWS - HEAD_DIM, n), 0)
    return jnp.where(r == 0, 1.0, 0.0).astype(BF16)

RW_CHUNK = 64
RW_GROUP = 2
GW = RW_GROUP * HEAD_DIM


def _cparams(sem):
    return pltpu.CompilerParams(dimension_semantics=sem, vmem_limit_bytes=VMEM_LIMIT_BYTES)


def _const_spec(shape):
    n = len(shape)
    return pl.BlockSpec(shape, lambda *_: (0,) * n, pipeline_mode=pl.Buffered(1))


def _sigmoid(z):
    return 1.0 / (1.0 + jnp.exp(-z))


def _softplus(z):
    return jnp.maximum(z, 0.0) + jnp.log1p(jnp.exp(-jnp.abs(z)))


def _modulated_norm(x, gain, shift, scale):
    ms = jnp.mean(x * x, axis=-1, keepdims=True)
    return (x * lax.rsqrt(ms + NORM_EPS) * gain) * (1.0 + scale) + shift


def _ada_body(c_ref, w_ref, b_ref, o_ref):
    c = c_ref[...]
    ca = c * _sigmoid(c)
    o_ref[0] = jnp.dot(ca, w_ref[0], precision=HIGHEST, preferred_element_type=F32) + b_ref[0]


def _ada(c, w_ada, b_ada):
    L, D, N6 = w_ada.shape
    B = c.shape[0]
    tn = 1536
    return pl.pallas_call(
        _ada_body,
        out_shape=jax.ShapeDtypeStruct((L, B, N6), F32),
        grid=(L, N6 // tn),
        in_specs=[pl.BlockSpec((B, D), lambda l, n: (0, 0)),
                  pl.BlockSpec((1, D, tn), lambda l, n: (l, 0, n)),
                  pl.BlockSpec((1, 1, tn), lambda l, n: (l, 0, n))],
        out_specs=pl.BlockSpec((1, B, tn), lambda l, n: (l, 0, n)),
        compiler_params=_cparams(("arbitrary", "arbitrary")),
        name="ada",
    )(c, w_ada, b_ada.reshape(L, 1, N6))


def _split3(t):
    hi = t.astype(BF16)
    mid = (t - hi.astype(F32)).astype(BF16)
    lo = (t - hi.astype(F32) - mid.astype(F32)).astype(BF16)
    return hi, mid, lo


def _rms(t, gain):
    return t * lax.rsqrt(jnp.mean(t * t, axis=-1, keepdims=True) + NORM_EPS) * gain


def _mla_q(q_lat, cos, sin, gq_ref, wq_ref, q_ref):
    qn = _rms(q_lat, gq_ref[...]).astype(BF16)
    q = jnp.dot(qn, wq_ref[...], preferred_element_type=F32)
    half = MLA_ROPE // 2
    q1, q2 = q[:, WIDTH:WIDTH + 128], q[:, WIDTH + 128:WIDTH + 256]
    qo1 = q1 * cos - q2 * sin
    qo2 = q1 * sin + q2 * cos
    scale = MLA_QK ** -0.5 * LOG2E
    for hd in range(N_HEADS):
        hs = slice(hd * HEAD_DIM, (hd + 1) * HEAD_DIM)
        ps = slice(hd * half, (hd + 1) * half)
        qh = jnp.concatenate([q[:, hs], qo1[:, ps], qo2[:, ps]], axis=-1) * scale
        q_ref[0, hd] = qh.astype(BF16)


def _mla_k(kvn, kr, cos, sin, wk_ref, k_ref):
    kn = jnp.dot(kvn, wk_ref[...], preferred_element_type=F32)
    half = MLA_ROPE // 2
    k1, k2 = kr[:, :half], kr[:, half:]
    c16, s16 = cos[:, :half], sin[:, :half]
    ko = jnp.concatenate([k1 * c16 - k2 * s16, k1 * s16 + k2 * c16], axis=-1)
    for hd in range(N_HEADS):
        k_ref[0, hd] = jnp.concatenate([kn[:, hd * HEAD_DIM:(hd + 1) * HEAD_DIM], ko], axis=-1).astype(BF16)


def _mla_v(kvn, wvt_ref, vt_ref):
    vt = lax.dot_general(wvt_ref[...], kvn, (((1,), (1,)), ((), ())), preferred_element_type=F32)
    for hd in range(N_HEADS):
        vt_ref[0, hd, 0:HEAD_DIM, :] = vt[hd * HEAD_DIM:(hd + 1) * HEAD_DIM, :].astype(BF16)
        vt_ref[0, hd, HEAD_DIM:VT_ROWS, :] = _ones_rows(vt.shape[1])


def _proj_body(x_ref, mod_ref, g_ref, w_ref, wvt_ref, mu_ref, bf_ref,
               pos_ref, gq_ref, gkv_ref, wq_ref, wk_ref, mwvt_ref, freq_ref,
               rkv_ref, small_ref, fq_ref, fk_ref, fvt_ref, gate_ref, mq_ref, mk_ref, mvt_ref,
               carry_ref, cc_ref, *, tm):
    j = pl.program_id(1)

    @pl.when(j == 0)
    def _():
        carry_ref[...] = jnp.zeros_like(carry_ref)
        cc_ref[...] = jnp.zeros_like(cc_ref)

    h = _modulated_norm(x_ref[0], g_ref[...], mod_ref[0, 0:1, :], mod_ref[0, 1:2, :])
    hb = h.astype(BF16)
    row = lax.broadcasted_iota(jnp.int32, (tm, 1), 0)

    ang = pos_ref[0].astype(F32) * freq_ref[...]
    cos, sin = jnp.cos(ang), jnp.sin(ang)

    def mm(c0, c1):
        return jnp.dot(hb, w_ref[:, c0:c1], preferred_element_type=F32)

    def shifted(c0, c1):
        res = mm(c0, c1)
        last = carry_ref[0:1, c0:c1]
        carry_ref[0:1, c0:c1] = res[tm - 1:tm, :]
        prev = jnp.where(row == 0, last, pltpu.roll(res, 1, 0))
        return res + (prev - res) * mu_ref[:, c0:c1]

    for i in range(3):
        c0 = C_RKV + i * WIDTH
        rkv_ref[0, :, c0:c0 + WIDTH] = shifted(c0, c0 + WIDTH).astype(BF16)
    small = shifted(C_SMALL, C_SMALL + SMALL_W)
    small_ref[0] = small

    z = small[:, S_FF:S_FF + N_HEADS] + bf_ref[...]
    logf = jnp.minimum(z, 0.0) - jnp.log1p(jnp.exp(-jnp.abs(z)))
    ri = lax.broadcasted_iota(jnp.int32, (tm, tm), 0)
    ci = lax.broadcasted_iota(jnp.int32, (tm, tm), 1)
    tri = jnp.where(ci <= ri, 1.0, 0.0).astype(BF16)
    f_hi, f_mid, f_lo = _split3(logf)
    cum = (jnp.dot(tri, f_hi, preferred_element_type=F32) + jnp.dot(tri, f_mid, preferred_element_type=F32)
           + jnp.dot(tri, f_lo, preferred_element_type=F32)) + cc_ref[0:1, 0:N_HEADS]
    cc_ref[0:1, 0:N_HEADS] = cum[tm - 1:tm, :]

    fq = mm(C_FOX, C_FOX + WIDTH) * (HEAD_DIM ** -0.5 * LOG2E)
    fk = mm(C_FOX + WIDTH, C_FOX + 2 * WIDTH)
    ew = N_HEADS * FOX_EXTRA
    rep = jnp.where(lax.broadcasted_iota(jnp.int32, (N_HEADS, ew), 0)
                    == lax.broadcasted_iota(jnp.int32, (N_HEADS, ew), 1) // FOX_EXTRA, 1.0, 0.0).astype(BF16)
    hi, mid, lo = (jnp.dot(t, rep, preferred_element_type=F32) for t in _split3(cum * LOG2E))
    lane = lax.broadcasted_iota(jnp.int32, (tm, ew), 1) % FOX_EXTRA
    eq_all = jnp.where(lane == 0, hi, jnp.where(lane == 1, mid, jnp.where(lane == 2, lo,
                       jnp.where(lane < 6, 1.0, 0.0))))
    ek_all = jnp.where(lane < 3, 1.0, jnp.where(lane == 3, -hi, jnp.where(lane == 4, -mid,
                       jnp.where(lane == 5, -lo, 0.0))))
    for hd in range(N_HEADS):
        hs = slice(hd * HEAD_DIM, (hd + 1) * HEAD_DIM)
        es = slice(hd * FOX_EXTRA, (hd + 1) * FOX_EXTRA)
        fq_ref[0, hd] = jnp.concatenate([fq[:, hs], eq_all[:, es]], axis=-1).astype(BF16)
        fk_ref[0, hd] = jnp.concatenate([fk[:, hs], ek_all[:, es]], axis=-1).astype(BF16)

    fvt = lax.dot_general(wvt_ref[...], hb, (((1,), (1,)), ((), ())), preferred_element_type=F32)
    for hd in range(N_HEADS):
        fvt_ref[0, hd, 0:HEAD_DIM, :] = fvt[hd * HEAD_DIM:(hd + 1) * HEAD_DIM, :].astype(BF16)
        fvt_ref[0, hd, HEAD_DIM:VT_ROWS, :] = _ones_rows(tm)

    lat = mm(C_LAT, C_GATE)
    kvn = _rms(lat[:, MLA_LORA:], gkv_ref[...]).astype(BF16)

    def gate_chunk(i):
        c0 = C_GATE + i * WIDTH
        gate_ref[0, :, i * WIDTH:(i + 1) * WIDTH] = _sigmoid(mm(c0, c0 + WIDTH)).astype(BF16)

    assert N_BRANCH * D_MODEL // WIDTH == 6
    gate_chunk(0)
    _mla_q(lat[:, :MLA_LORA], cos, sin, gq_ref, wq_ref, mq_ref)
    gate_chunk(1)
    gate_chunk(2)
    _mla_k(kvn, small[:, S_KR:S_KR + MLA_ROPE], cos, sin, wk_ref, mk_ref)
    gate_chunk(3)
    gate_chunk(4)
    _mla_v(kvn, mwvt_ref, mvt_ref)
    gate_chunk(5)


def _proj(x, mod, gain, w_cat, w_fv_t, mu, b_f, positions, gq, gkv, wq, wk, wv_t, freq, *, tm=256):
    B, T, D = x.shape
    nt = T // tm
    row3 = lambda b, j: (b, j, 0)
    hm = lambda d: (jax.ShapeDtypeStruct((B, N_HEADS, T, d), BF16),
                    pl.BlockSpec((1, N_HEADS, tm, d), lambda b, j: (b, 0, j, 0)))
    vt = (jax.ShapeDtypeStruct((B, N_HEADS, VT_ROWS, T), BF16),
          pl.BlockSpec((1, N_HEADS, VT_ROWS, tm), lambda b, j: (b, 0, 0, j)))
    outs = [(jax.ShapeDtypeStruct((B, T, 3 * WIDTH), BF16), pl.BlockSpec((1, tm, 3 * WIDTH), row3)),
            (jax.ShapeDtypeStruct((B, T, SMALL_W), F32), pl.BlockSpec((1, tm, SMALL_W), row3)),
            hm(FOX_QK), hm(FOX_QK), vt,
            (jax.ShapeDtypeStruct((B, T, N_BRANCH * D), BF16),
             pl.BlockSpec((1, tm, N_BRANCH * D), row3)),
            hm(MLA_QK), hm(MLA_QK), vt]
    consts = [gain, w_cat, w_fv_t, mu, b_f]
    mla_consts = [gq, gkv, wq, wk, wv_t, freq]
    return pl.pallas_call(
        functools.partial(_proj_body, tm=tm),
        out_shape=tuple(o[0] for o in outs),
        grid=(B, nt),
        in_specs=([pl.BlockSpec((1, tm, D), row3), pl.BlockSpec((1, 6, D), lambda b, j: (b, 0, 0))]
                  + [_const_spec(t.shape) for t in consts]
                  + [pl.BlockSpec((1, tm, 1), row3)]
                  + [_const_spec(t.shape) for t in mla_consts]),
        out_specs=tuple(o[1] for o in outs),
        scratch_shapes=[pltpu.VMEM((8, N_SHIFT), F32),
                        pltpu.VMEM((8, 128), F32)],
        compiler_params=_cparams(("arbitrary", "arbitrary")),
        name="proj",
    )(x, mod, *consts, positions, *mla_consts)


def _rwkv_body(*refs, has_vres, n_chunks):
    if has_vres:
        (rkv_ref, small_ref, vfirst_ref, lw_ref, la_ref, lg_ref, lv_ref, vec_ref, ones_ref,
         y_ref, ht_ref) = refs
    else:
        rkv_ref, small_ref, lw_ref, la_ref, lg_ref, vec_ref, ones_ref, y_ref, ht_ref = refs
    C = RW_CHUNK
    TC = n_chunks * C
    j = pl.program_id(1)

    @pl.when(j == 0)
    def _():
        ht_ref[...] = jnp.zeros_like(ht_ref)

    def vec(i):
        return vec_ref[i:i + 1, :]

    w0, a0, k_k, k_a, r_k, gn_w, gn_b, v0 = (vec(i) for i in range(8))
    ones_bd = ones_ref[...]
    HW = ones_bd.shape[0]

    def headsum(t):
        tb = t.astype(BF16)
        return jnp.concatenate([jnp.dot(tb[:, c0:c0 + HW], ones_bd, preferred_element_type=F32)
                                for c0 in range(0, WIDTH, HW)], axis=1)

    def lora(t, w_ref):
        return jnp.dot(t.astype(BF16), w_ref[...], preferred_element_type=F32)

    r = rkv_ref[0, :, 0:WIDTH].astype(F32)
    k = rkv_ref[0, :, WIDTH:2 * WIDTH].astype(F32)
    v = rkv_ref[0, :, 2 * WIDTH:3 * WIDTH].astype(F32)
    sm = small_ref[0]
    wl = sm[:, S_WL:S_WL + RW_DECAY_LORA]
    al = sm[:, S_AL:S_AL + RW_A_LORA]
    gl = sm[:, S_GL:S_GL + RW_G_LORA]

    w = -_softplus(-(w0 + lora(jnp.tanh(wl), lw_ref))) - 0.5
    logd = -jnp.exp(w)
    a = _sigmoid(a0 + lora(al, la_ref))
    g = lora(_sigmoid(gl), lg_ref)
    if has_vres:
        lvv = sm[:, S_VR:S_VR + RW_V_LORA]
        v = v + (vfirst_ref[0].astype(F32) - v) * _sigmoid(v0 + lora(lvv, lv_ref))
    kk = k * k_k
    kk = kk * lax.rsqrt(jnp.maximum(headsum(kk * kk), 1e-24))
    k2 = k * (1.0 + (a - 1.0) * k_a)

    ri = lax.broadcasted_iota(jnp.int32, (TC, TC), 0)
    ci = lax.broadcasted_iota(jnp.int32, (TC, TC), 1)
    tri = jnp.where((ci <= ri) & ((ri // C) == (ci // C)), 1.0, 0.0).astype(BF16)
    ld_hi, ld_mid, ld_lo = _split3(logd)
    cl = (jnp.dot(tri, ld_hi, preferred_element_type=F32) + jnp.dot(tri, ld_mid, preferred_element_type=F32)
          + jnp.dot(tri, ld_lo, preferred_element_type=F32))
    cl_last = jnp.concatenate([jnp.broadcast_to(cl[(c + 1) * C - 1:(c + 1) * C, :], (C, WIDTH))
                               for c in range(n_chunks)], axis=0)
    e_neg = jnp.exp(-cl)
    e_end = jnp.exp(cl_last - cl)
    ka = kk * a
    a_t = kk * jnp.exp(cl - logd)
    r_t = r * jnp.exp(cl)
    b_m = -ka * e_neg
    k_m = k2 * e_neg
    b_p = -ka * e_end
    k_p = k2 * e_end
    p_c = jnp.exp(cl_last)

    bi = lax.broadcasted_iota(jnp.int32, (GW, GW), 0)
    bj = lax.broadcasted_iota(jnp.int32, (GW, GW), 1)
    bd_mask = (bi // HEAD_DIM) == (bj // HEAD_DIM)
    strict = bi > bj
    incl = bi >= bj
    eye = (bi == bj).astype(F32)

    def bd(t):
        return jnp.where(bd_mask, jnp.concatenate([t] * RW_GROUP, axis=0), 0.0)

    def collapse(t):
        return sum(t[hh * C:(hh + 1) * C, :] for hh in range(RW_GROUP))

    def dot16(p, q):
        return jnp.dot(p.astype(BF16), q.astype(BF16), preferred_element_type=F32)

    chains = [(ch, gi) for ch in range(n_chunks) for gi in range(N_HEADS // RW_GROUP)]
    nc = len(chains)

    def piece(t, c):
        ch, gi = chains[c]
        return t[ch * C:(ch + 1) * C, gi * GW:(gi + 1) * GW]

    def bds(t):
        return [bd(piece(t, c)).astype(BF16) for c in range(nc)]

    at_bd, rt_bd, bm_bd, km_bd, bp_bd, kp_bd, v_bd = (bds(t) for t in (a_t, r_t, b_m, k_m, b_p, k_p, v))
    a_all = [lax.dot_general(jnp.concatenate([at_bd[c], rt_bd[c]], axis=0),
                             jnp.concatenate([bm_bd[c], km_bd[c]], axis=0),
                             (((1,), (1,)), ((), ())), preferred_element_type=F32) for c in range(nc)]
    a_ab = [jnp.where(strict, a_all[c][:GW, :GW], 0.0) for c in range(nc)]
    a_ak = [jnp.where(strict, a_all[c][:GW, GW:], 0.0).astype(BF16) for c in range(nc)]
    m_rb = [jnp.where(incl, a_all[c][GW:, :GW], 0.0).astype(BF16) for c in range(nc)]
    m_rk = [jnp.where(incl, a_all[c][GW:, GW:], 0.0).astype(BF16) for c in range(nc)]

    pw = [t.astype(BF16) for t in a_ab]
    tinv = [eye + t for t in a_ab]
    n_sq = int(np.log2(C)) - 1
    for it in range(n_sq):
        pw = [jnp.dot(pw[c], pw[c], preferred_element_type=F32).astype(BF16) for c in range(nc)]
        tinv = [tinv[c] + dot16(tinv[c], pw[c]) for c in range(nc)]
    tinv = [t.astype(BF16) for t in tinv]

    akv = [dot16(a_ak[c], v_bd[c]) for c in range(nc)]
    x = [dot16(tinv[c], jnp.concatenate([at_bd[c], akv[c].astype(BF16)], axis=1)) for c in range(nc)]
    wm_bd = [t[:, :GW].astype(BF16) for t in x]
    u0_bd = [t[:, GW:].astype(BF16) for t in x]
    y1_bd = [dot16(jnp.concatenate([m_rk[c], m_rb[c]], axis=1),
                   jnp.concatenate([v_bd[c], u0_bd[c]], axis=0)) for c in range(nc)]
    rm_bd = [dot16(m_rb[c], wm_bd[c]) for c in range(nc)]
    qt = [dot16(x[c][:, :GW].T, bp_bd[c]) for c in range(nc)]
    h1t = [dot16(jnp.concatenate([x[c][:, GW:], v_bd[c].astype(F32)], axis=0).T,
                 jnp.concatenate([bp_bd[c], kp_bd[c]], axis=0)) for c in range(nc)]
    local = {}
    for c, (ch, gi) in enumerate(chains):
        local[ch, gi] = (collapse(y1_bd[c]), (piece(r_t, c) + collapse(rm_bd[c])).astype(BF16),
                         qt[c].astype(BF16), h1t[c], p_c[ch * C:ch * C + 1, gi * GW:(gi + 1) * GW])

    y_rows = []
    for ch in range(n_chunks):
        ys = []
        for gi in range(N_HEADS // RW_GROUP):
            y1, rm, qt, h1t, pc = local[ch, gi]
            ht = ht_ref[gi]
            htb = ht.astype(BF16)
            ys.append(y1 + lax.dot_general(rm, htb, (((1,), (1,)), ((), ())), preferred_element_type=F32))
            ht_ref[gi] = ht * pc + jnp.dot(htb, qt, preferred_element_type=F32) + h1t
        y_rows.append(jnp.concatenate(ys, axis=1))
    y = jnp.concatenate(y_rows, axis=0)
    inv_n = 1.0 / HEAD_DIM
    mean = headsum(y) * inv_n
    dlt = y - mean
    var = headsum(dlt * dlt) * inv_n
    y = dlt * lax.rsqrt(var + RW_GN_EPS) * gn_w + gn_b
    y = y + headsum(r * k2 * r_k) * v
    y_ref[0] = (y * g).astype(BF16)


def _rwkv(rkv, small, vfirst_rkv, lw, la, lg, lv, vecs, ones_bd, *, n_chunks=4):
    B, T, _ = rkv.shape
    C = n_chunks * RW_CHUNK
    has_vres = vfirst_rkv is not None
    row3 = lambda b, j: (b, j, 0)
    in_specs = [pl.BlockSpec((1, C, 3 * WIDTH), row3), pl.BlockSpec((1, C, SMALL_W), row3)]
    args = [rkv, small]
    if has_vres:
        in_specs.append(pl.BlockSpec((1, C, WIDTH), lambda b, j: (b, j, 2)))
        args.append(vfirst_rkv)
    w_list = [lw, la, lg] + ([lv] if has_vres else []) + [vecs, ones_bd]
    in_specs += [_const_spec(t.shape) for t in w_list]
    args += w_list
    return pl.pallas_call(
        functools.partial(_rwkv_body, has_vres=has_vres, n_chunks=n_chunks),
        out_shape=jax.ShapeDtypeStruct((B, T, WIDTH), BF16),
        grid=(B, T // C),
        in_specs=in_specs,
        out_specs=pl.BlockSpec((1, C, WIDTH), row3),
        scratch_shapes=[pltpu.VMEM((N_HEADS // RW_GROUP, GW, GW), F32)],
        compiler_params=_cparams(("arbitrary", "arbitrary")),
        name="rwkv",
    )(*args)


def _attn_body(q_ref, k_ref, vt_ref, o_ref, m_ref, acc_ref, *, tq, tks):
    qi = pl.program_id(2)
    tk = tq
    hps = q_ref.shape[1]

    m_ref[...] = jnp.full_like(m_ref, NEG_BIG)
    acc_ref[...] = jnp.zeros_like(acc_ref)

    def step(ki, diag):
        k0 = pl.multiple_of(ki * tk, tk)
        nsub = tk // tks
        units = [(i, s) for s in range(nsub) for i in range(hps)]

        def lo_of(s):
            return s * tks if diag else 0

        def qk(i, s):
            kb = k_ref[0, i, pl.ds(k0 + s * tks, tks), :]
            st = lax.dot_general(kb, q_ref[0, i, lo_of(s):, :], (((1,), (1,)), ((), ())),
                                 preferred_element_type=F32)
            if diag:
                key = lax.broadcasted_iota(jnp.int32, st.shape, 0)
                qry = lax.broadcasted_iota(jnp.int32, st.shape, 1)
                st = jnp.where(key <= qry, st, NEG_BIG)
            return st

        def softmax(i, s, st):
            lo = lo_of(s)
            m_old = m_ref[i, :, lo:]
            m_new = jnp.maximum(m_old, jnp.max(st, axis=0, keepdims=True))
            m_ref[i, :, lo:] = m_new
            return jnp.exp2(st - m_new).astype(BF16), jnp.exp2(m_old - m_new)

        def pv(i, s, p, alpha):
            lo = lo_of(s)
            vtb = vt_ref[0, i, :, pl.ds(k0 + s * tks, tks)]
            acc_ref[i, :, lo:] = alpha * acc_ref[i, :, lo:] + jnp.dot(vtb, p, preferred_element_type=F32)

        n = len(units)
        sc = {u: qk(*units[u]) for u in range(min(2, n))}
        pr = {}
        for u in range(n):
            pr[u] = softmax(*units[u], sc.pop(u))
            if u + 2 < n:
                sc[u + 2] = qk(*units[u + 2])
            if u >= 1:
                pv(*units[u - 1], *pr.pop(u - 1))
        pv(*units[n - 1], *pr.pop(n - 1))

    def loop_body(ki, carry):
        step(ki, False)
        return carry

    lax.fori_loop(0, qi, loop_body, 0)
    step(qi, True)

    out_t = jnp.concatenate([acc_ref[i, 0:HEAD_DIM, :] / acc_ref[i, HEAD_DIM:HEAD_DIM + 1, :]
                             for i in range(hps)], axis=0)
    o_ref[0] = out_t.T.astype(BF16)


def _attn(q, k, vt, *, name, tq=512, tks=256, hps=8):
    B, H, T, dk = q.shape
    nq = T // tq
    return pl.pallas_call(
        functools.partial(_attn_body, tq=tq, tks=tks),
        out_shape=jax.ShapeDtypeStruct((B, T, H * HEAD_DIM), BF16),
        grid=(B, H // hps, nq),
        in_specs=[pl.BlockSpec((1, hps, tq, dk), lambda b, p, i: (b, p, i, 0)),
                  pl.BlockSpec((1, hps, T, dk), lambda b, p, i: (b, p, 0, 0)),
                  pl.BlockSpec((1, hps, VT_ROWS, T), lambda b, p, i: (b, p, 0, 0))],
        out_specs=pl.BlockSpec((1, tq, hps * HEAD_DIM), lambda b, p, i: (b, i, p)),
        scratch_shapes=[pltpu.VMEM((hps, 1, tq), F32), pltpu.VMEM((hps, VT_ROWS, tq), F32)],
        compiler_params=_cparams(("arbitrary", "arbitrary", "arbitrary")),
        name=name,
    )(q, k, vt)


def _merge_body(yr_ref, yf_ref, ym_ref, gate_ref, x_ref, mod_ref, wb_ref, wo_ref, o_ref):
    D = D_MODEL
    mixed = None
    for gi, y_ref in enumerate((yr_ref, yf_ref, ym_ref)):
        bd = jnp.dot(y_ref[0], wb_ref[gi], preferred_element_type=F32)
        term = gate_ref[0, :, gi * D:(gi + 1) * D].astype(F32) * bd
        mixed = term if mixed is None else mixed + term
    out = jnp.dot(mixed.astype(BF16), wo_ref[...], preferred_element_type=F32)
    o_ref[0] = x_ref[0] + mod_ref[0, 2:3, :] * out


def _merge(y_rw, y_fox, y_mla, gates, x, mod, wb, wo, *, tm=1024):
    B, T, D = x.shape
    row3 = lambda b, j: (b, j, 0)
    yspec = pl.BlockSpec((1, tm, WIDTH), row3)
    return pl.pallas_call(
        _merge_body,
        out_shape=jax.ShapeDtypeStruct((B, T, D), F32),
        grid=(B, T // tm),
        in_specs=[yspec, yspec, yspec,
                  pl.BlockSpec((1, tm, N_BRANCH * D), row3),
                  pl.BlockSpec((1, tm, D), row3),
                  pl.BlockSpec((1, 6, D), lambda b, j: (b, 0, 0)),
                  _const_spec(wb.shape), _const_spec(wo.shape)],
        out_specs=pl.BlockSpec((1, tm, D), row3),
        compiler_params=_cparams(("arbitrary", "arbitrary")),
        name="merge",
    )(y_rw, y_fox, y_mla, gates, x, mod, wb, wo)


def _ffn_body(x_ref, mod_ref, g_ref, wup_ref, cw_ref, cb_ref, wdn_ref, fg_ref, o_ref, carry_ref,
              *, tm, fc, final):
    j = pl.program_id(1)

    @pl.when(j == 0)
    def _():
        carry_ref[...] = jnp.zeros_like(carry_ref)

    x = x_ref[0]
    hb = _modulated_norm(x, g_ref[...], mod_ref[0, 3:4, :], mod_ref[0, 4:5, :]).astype(BF16)
    row8 = lax.broadcasted_iota(jnp.int32, (8, 1), 0)

    def up(c0):
        return jnp.dot(hb, wup_ref[:, c0:c0 + fc], preferred_element_type=F32)

    def conv(u, c0):
        t2 = carry_ref[0:1, c0:c0 + fc]
        t1 = carry_ref[1:2, c0:c0 + fc]
        carry_ref[0:2, c0:c0 + fc] = u[tm - 2:tm, :]
        r1 = pltpu.roll(u, 1, 0)
        r2 = pltpu.roll(u, 2, 0)
        p1 = jnp.concatenate([jnp.where(row8 == 0, t1, r1[:8]), r1[8:]], axis=0)
        p2 = jnp.concatenate([jnp.where(row8 == 0, t2, jnp.where(row8 == 1, t1, r2[:8])), r2[8:]], axis=0)
        return (u * cw_ref[2:3, c0:c0 + fc] + p1 * cw_ref[1:2, c0:c0 + fc]
                + p2 * cw_ref[0:1, c0:c0 + fc] + cb_ref[:, c0:c0 + fc])

    n_ff = D_FF // fc
    raw = {0: (up(0), up(D_FF))}
    acts = {}
    acc = None

    def down(c):
        part = jnp.dot(acts.pop(c), wdn_ref[c * fc:(c + 1) * fc, :], preferred_element_type=F32)
        return part if acc is None else acc + part

    for c in range(n_ff):
        rg, rv = raw.pop(c)
        if c + 1 < n_ff:
            nxt_g = up((c + 1) * fc)
        ug = conv(rg, c * fc)
        gate = ug * _sigmoid(ug)
        if c + 1 < n_ff:
            raw[c + 1] = (nxt_g, up(D_FF + (c + 1) * fc))
        uv = conv(rv, D_FF + c * fc)
        acts[c] = (gate * uv).astype(BF16)
        if c >= 1:
            acc = down(c - 1)
    acc = down(n_ff - 1)
    out = x + mod_ref[0, 5:6, :] * acc
    if final:
        out = out * lax.rsqrt(jnp.mean(out * out, axis=-1, keepdims=True) + NORM_EPS) * fg_ref[...]
    o_ref[0] = out


def _ffn(x, mod, gain, w_up, conv_w, conv_b, w_down, final_g, *, final, tm=512, fc=256):
    B, T, D = x.shape
    row3 = lambda b, j: (b, j, 0)
    return pl.pallas_call(
        functools.partial(_ffn_body, tm=tm, fc=fc, final=final),
        out_shape=jax.ShapeDtypeStruct((B, T, D), F32),
        grid=(B, T // tm),
        in_specs=[pl.BlockSpec((1, tm, D), row3),
                  pl.BlockSpec((1, 6, D), lambda b, j: (b, 0, 0)),
                  _const_spec(gain.shape), _const_spec(w_up.shape), _const_spec(conv_w.shape),
                  _const_spec(conv_b.shape), _const_spec(w_down.shape), _const_spec(final_g.shape)],
        out_specs=pl.BlockSpec((1, tm, D), row3),
        scratch_shapes=[pltpu.VMEM((8, 2 * D_FF), F32)],
        compiler_params=_cparams(("arbitrary", "arbitrary")),
        name="ffn",
    )(x, mod, gain, w_up, conv_w, conv_b, w_down, final_g)


def _cat_weight(w_in_i, vres_down_i):
    D = w_in_i.shape[0]
    o_wl, o_k, o_v, o_al, o_gl, rw_cols = 512, 576, 1088, 1600, 1664, 1792
    o_fox = rw_cols
    o_ff = o_fox + 3 * WIDTH
    o_mq = o_ff + N_HEADS
    o_kr = o_mq + 2 * MLA_LORA
    o_gate = o_kr + MLA_ROPE
    vres = jnp.zeros((D, RW_V_LORA), w_in_i.dtype) if vres_down_i is None else vres_down_i
    pad = jnp.zeros((D, SMALL_W - (S_FF + N_HEADS)), w_in_i.dtype)
    cols = [w_in_i[:, 0:512], w_in_i[:, o_k:o_k + 512], w_in_i[:, o_v:o_v + 512],
            w_in_i[:, o_wl:o_wl + 64], w_in_i[:, o_al:o_al + 64], w_in_i[:, o_gl:o_gl + 128],
            vres, w_in_i[:, o_kr:o_kr + MLA_ROPE], w_in_i[:, o_ff:o_ff + N_HEADS], pad,
            w_in_i[:, o_fox:o_fox + 2 * WIDTH],
            w_in_i[:, o_mq:o_mq + 2 * MLA_LORA],
            w_in_i[:, o_gate:o_gate + N_BRANCH * D_MODEL]]
    w_fv_t = w_in_i[:, o_fox + 2 * WIDTH:o_fox + 3 * WIDTH].T
    return jnp.concatenate(cols, axis=1).astype(BF16), w_fv_t.astype(BF16)


def _cat_mu(mu_i, vres_mu_i):
    o_wl, o_k, o_v, o_al, o_gl = 512, 576, 1088, 1600, 1664
    vres = jnp.zeros((RW_V_LORA,), F32) if vres_mu_i is None else vres_mu_i
    tail = jnp.zeros((SMALL_W - S_KR,), F32)
    return jnp.concatenate([mu_i[0:512], mu_i[o_k:o_k + 512], mu_i[o_v:o_v + 512],
                            mu_i[o_wl:o_wl + 64], mu_i[o_al:o_al + 64], mu_i[o_gl:o_gl + 128],
                            vres, tail]).reshape(1, N_SHIFT)


def _mla_weights(w_q_up, w_kv_up):
    half = MLA_ROPE // 2
    wq = w_q_up.reshape(MLA_LORA, N_HEADS, MLA_QK)
    wq = jnp.concatenate([wq[:, :, :MLA_NOPE].reshape(MLA_LORA, -1),
                          wq[:, :, MLA_NOPE:MLA_NOPE + half].reshape(MLA_LORA, -1),
                          wq[:, :, MLA_NOPE + half:].reshape(MLA_LORA, -1)], axis=1)
    wkv = w_kv_up.reshape(MLA_LORA, N_HEADS, MLA_NOPE + HEAD_DIM)
    wk = wkv[:, :, :MLA_NOPE].reshape(MLA_LORA, -1)
    wv_t = wkv[:, :, MLA_NOPE:].reshape(MLA_LORA, -1).T
    return wq.astype(BF16), wk.astype(BF16), wv_t.astype(BF16)


def kernel(x, c, positions, norm1_g, norm2_g, w_ada, b_ada, w_in, mu_shift, rw_w_up, rw_w0, rw_a_up, rw_a0, rw_g_up, rw_k_k, rw_k_a, rw_r_k, rw_gn_w, rw_gn_b, rw_vres_down, rw_vres_mu, rw_vres_up, rw_v0, fox_b_f, mla_q_norm_g, mla_w_q_up, mla_kv_norm_g, mla_w_kv_up, w_branch, w_o, ffn_w_up, ffn_conv_w, ffn_conv_b, ffn_w_down, final_g):
    B, T, D = x.shape
    depth = w_in.shape[0]
    half = MLA_ROPE // 2
    inv_freq = np.power(np.float32(ROPE_BASE), -np.arange(half, dtype=np.float32) / np.float32(half))
    freq = jnp.asarray(np.tile(inv_freq.astype(np.float32), 128 // half).reshape(1, 128))
    hid = np.arange(WIDTH // 2) // HEAD_DIM
    ones_bd = jnp.asarray((hid[:, None] == hid[None, :]).astype(np.float32), BF16)
    pos3 = positions.reshape(B, T, 1)

    mods = _ada(c, w_ada, b_ada).reshape(depth, B, 6, D)
    rkv0 = None
    for i in range(depth):
        mod = mods[i]
        w_cat, w_fv_t = _cat_weight(w_in[i], None if i == 0 else rw_vres_down[i - 1])
        mu = _cat_mu(mu_shift[i], None if i == 0 else rw_vres_mu[i - 1])
        wq, wk, wv_t = _mla_weights(mla_w_q_up[i], mla_w_kv_up[i])
        rkv, small, fq, fk, fvt, gates, mq, mk, mvt = _proj(
            x, mod, norm1_g[i].reshape(1, D), w_cat, w_fv_t, mu, fox_b_f[i].reshape(1, N_HEADS),
            pos3, mla_q_norm_g[i].reshape(1, -1), mla_kv_norm_g[i].reshape(1, -1), wq, wk, wv_t, freq)

        zero = jnp.zeros((WIDTH,), F32)
        vecs = jnp.stack([rw_w0[i], rw_a0[i], rw_k_k[i], rw_k_a[i], rw_r_k[i], rw_gn_w[i], rw_gn_b[i],
                          zero if i == 0 else rw_v0[i - 1]])
        y_rw = _rwkv(rkv, small, None if i == 0 else rkv0,
                     rw_w_up[i].astype(BF16), rw_a_up[i].astype(BF16), rw_g_up[i].astype(BF16),
                     None if i == 0 else rw_vres_up[i - 1].astype(BF16), vecs, ones_bd)
        if i == 0:
            rkv0 = rkv

        y_fox = _attn(fq, fk, fvt, name="attn_fox")

        y_mla = _attn(mq, mk, mvt, name="attn_mla")

        x = _merge(y_rw, y_fox, y_mla, gates, x, mod,
                   w_branch[i].astype(BF16), w_o[i].astype(BF16))
        x = _ffn(x, mod, norm2_g[i].reshape(1, D), ffn_w_up[i].astype(BF16), ffn_conv_w[i],
                 ffn_conv_b[i].reshape(1, -1), ffn_w_down[i].astype(BF16), final_g.reshape(1, D),
                 final=(i == depth - 1))
    return x
```

```python
import functools

import numpy as np
import jax
import jax.numpy as jnp
from jax import lax
from jax.experimental import pallas as pl
from jax.experimental.pallas import tpu as pltpu

F32 = jnp.float32
BF16 = jnp.bfloat16
HIGHEST = lax.Precision.HIGHEST

D_MODEL = 1024
HEAD_DIM = 64
N_HEADS = 8
WIDTH = N_HEADS * HEAD_DIM
RW_DECAY_LORA = 64
RW_A_LORA = 64
RW_V_LORA = 32
RW_G_LORA = 128
RW_GN_EPS = 64e-5
MLA_NOPE = 64
MLA_ROPE = 32
MLA_QK = MLA_NOPE + MLA_ROPE
MLA_LORA = 256
ROPE_BASE = 10000.0
N_BRANCH = 3
D_FF = 2816
NORM_EPS = 1e-6
NEG_BIG = -1e30

VMEM_LIMIT_BYTES = 56 * 1024 * 1024

C_RKV = 0
C_SMALL = 3 * WIDTH
SMALL_W = 384
S_WL, S_AL, S_GL, S_VR, S_KR, S_FF = 0, 64, 128, 256, 288, 320
C_FOX = C_SMALL + SMALL_W
C_LAT = C_FOX + 2 * WIDTH
C_GATE = C_LAT + 2 * MLA_LORA
N_COLS = C_GATE + N_BRANCH * D_MODEL
N_SHIFT = C_FOX

LOG2E = 1.4426950408889634
FOX_EXTRA = 8
FOX_QK = HEAD_DIM + FOX_EXTRA
VT_ROWS = HEAD_DIM + 16


def _ones_rows(n):
    r = lax.broadcasted_iota(jnp.int32, (VT_ROWS - HEAD_DIM, n), 0)
    return jnp.where(r == 0, 1.0, 0.0).astype(BF16)

RW_CHUNK = 64
RW_GROUP = 2
GW = RW_GROUP * HEAD_DIM


def _cparams(sem):
    return pltpu.CompilerParams(dimension_semantics=sem, vmem_limit_bytes=VMEM_LIMIT_BYTES)


def _const_spec(shape):
    n = len(shape)
    return pl.BlockSpec(shape, lambda *_: (0,) * n, pipeline_mode=pl.Buffered(1))


def _sigmoid(z):
    return 1.0 / (1.0 + jnp.exp(-z))


def _softplus(z):
    return jnp.maximum(z, 0.0) + jnp.log1p(jnp.exp(-jnp.abs(z)))


def _modulated_norm(x, gain, shift, scale):
    ms = jnp.mean(x * x, axis=-1, keepdims=True)
    return (x * lax.rsqrt(ms + NORM_EPS) * gain) * (1.0 + scale) + shift


def _ada_body(c_ref, w_ref, b_ref, o_ref):
    c = c_ref[...]
    ca = c * _sigmoid(c)
    o_ref[0] = jnp.dot(ca, w_ref[0], precision=HIGHEST, preferred_element_type=F32) + b_ref[0]


def _ada(c, w_ada, b_ada):
    L, D, N6 = w_ada.shape
    B = c.shape[0]
    tn = 1536
    return pl.pallas_call(
        _ada_body,
        out_shape=jax.ShapeDtypeStruct((L, B, N6), F32),
        grid=(L, N6 // tn),
        in_specs=[pl.BlockSpec((B, D), lambda l, n: (0, 0)),
                  pl.BlockSpec((1, D, tn), lambda l, n: (l, 0, n)),
                  pl.BlockSpec((1, 1, tn), lambda l, n: (l, 0, n))],
        out_specs=pl.BlockSpec((1, B, tn), lambda l, n: (l, 0, n)),
        compiler_params=_cparams(("arbitrary", "arbitrary")),
        name="ada",
    )(c, w_ada, b_ada.reshape(L, 1, N6))


def _split3(t):
    hi = t.astype(BF16)
    mid = (t - hi.astype(F32)).astype(BF16)
    lo = (t - hi.astype(F32) - mid.astype(F32)).astype(BF16)
    return hi, mid, lo


def _rms(t, gain):
    return t * lax.rsqrt(jnp.mean(t * t, axis=-1, keepdims=True) + NORM_EPS) * gain


def _mla_q(q_lat, cos, sin, gq_ref, wq_ref, q_ref):
    qn = _rms(q_lat, gq_ref[...]).astype(BF16)
    q = jnp.dot(qn, wq_ref[...], preferred_element_type=F32)
    half = MLA_ROPE // 2
    q1, q2 = q[:, WIDTH:WIDTH + 128], q[:, WIDTH + 128:WIDTH + 256]
    qo1 = q1 * cos - q2 * sin
    qo2 = q1 * sin + q2 * cos
    scale = MLA_QK ** -0.5 * LOG2E
    for hd in range(N_HEADS):
        hs = slice(hd * HEAD_DIM, (hd + 1) * HEAD_DIM)
        ps = slice(hd * half, (hd + 1) * half)
        qh = jnp.concatenate([q[:, hs], qo1[:, ps], qo2[:, ps]], axis=-1) * scale
        q_ref[0, hd] = qh.astype(BF16)


def _mla_k(kvn, kr, cos, sin, wk_ref, k_ref):
    kn = jnp.dot(kvn, wk_ref[...], preferred_element_type=F32)
    half = MLA_ROPE // 2
    k1, k2 = kr[:, :half], kr[:, half:]
    c16, s16 = cos[:, :half], sin[:, :half]
    ko = jnp.concatenate([k1 * c16 - k2 * s16, k1 * s16 + k2 * c16], axis=-1)
    for hd in range(N_HEADS):
        k_ref[0, hd] = jnp.concatenate([kn[:, hd * HEAD_DIM:(hd + 1) * HEAD_DIM], ko], axis=-1).astype(BF16)


def _mla_v(kvn, wvt_ref, vt_ref):
    vt = lax.dot_general(wvt_ref[...], kvn, (((1,), (1,)), ((), ())), preferred_element_type=F32)
    for hd in range(N_HEADS):
        vt_ref[0, hd, 0:HEAD_DIM, :] = vt[hd * HEAD_DIM:(hd + 1) * HEAD_DIM, :].astype(BF16)
        vt_ref[0, hd, HEAD_DIM:VT_ROWS, :] = _ones_rows(vt.shape[1])


def _proj_body(x_ref, mod_ref, g_ref, w_ref, wvt_ref, mu_ref, bf_ref,
               pos_ref, gq_ref, gkv_ref, wq_ref, wk_ref, mwvt_ref, freq_ref,
               rkv_ref, small_ref, fq_ref, fk_ref, fvt_ref, gate_ref, mq_ref, mk_ref, mvt_ref,
               carry_ref, cc_ref, *, tm):
    j = pl.program_id(1)

    @pl.when(j == 0)
    def _():
        carry_ref[...] = jnp.zeros_like(carry_ref)
        cc_ref[...] = jnp.zeros_like(cc_ref)

    h = _modulated_norm(x_ref[0], g_ref[...], mod_ref[0, 0:1, :], mod_ref[0, 1:2, :])
    hb = h.astype(BF16)
    row = lax.broadcasted_iota(jnp.int32, (tm, 1), 0)

    ang = pos_ref[0].astype(F32) * freq_ref[...]
    cos, sin = jnp.cos(ang), jnp.sin(ang)

    def mm(c0, c1):
        return jnp.dot(hb, w_ref[:, c0:c1], preferred_element_type=F32)

    def shifted(c0, c1):
        res = mm(c0, c1)
        last = carry_ref[0:1, c0:c1]
        carry_ref[0:1, c0:c1] = res[tm - 1:tm, :]
        prev = jnp.where(row == 0, last, pltpu.roll(res, 1, 0))
        return res + (prev - res) * mu_ref[:, c0:c1]

    for i in range(3):
        c0 = C_RKV + i * WIDTH
        rkv_ref[0, :, c0:c0 + WIDTH] = shifted(c0, c0 + WIDTH).astype(BF16)
    small = shifted(C_SMALL, C_SMALL + SMALL_W)
    small_ref[0] = small

    z = small[:, S_FF:S_FF + N_HEADS] + bf_ref[...]
    logf = jnp.minimum(z, 0.0) - jnp.log1p(jnp.exp(-jnp.abs(z)))
    ri = lax.broadcasted_iota(jnp.int32, (tm, tm), 0)
    ci = lax.broadcasted_iota(jnp.int32, (tm, tm), 1)
    tri = jnp.where(ci <= ri, 1.0, 0.0).astype(BF16)
    f_hi, f_mid, f_lo = _split3(logf)
    cum = (jnp.dot(tri, f_hi, preferred_element_type=F32) + jnp.dot(tri, f_mid, preferred_element_type=F32)
           + jnp.dot(tri, f_lo, preferred_element_type=F32)) + cc_ref[0:1, 0:N_HEADS]
    cc_ref[0:1, 0:N_HEADS] = cum[tm - 1:tm, :]

    fq = mm(C_FOX, C_FOX + WIDTH) * (HEAD_DIM ** -0.5 * LOG2E)
    fk = mm(C_FOX + WIDTH, C_FOX + 2 * WIDTH)
    ew = N_HEADS * FOX_EXTRA
    rep = jnp.where(lax.broadcasted_iota(jnp.int32, (N_HEADS, ew), 0)
                    == lax.broadcasted_iota(jnp.int32, (N_HEADS, ew), 1) // FOX_EXTRA, 1.0, 0.0).astype(BF16)
    hi, mid, lo = (jnp.dot(t, rep, preferred_element_type=F32) for t in _split3(cum * LOG2E))
    lane = lax.broadcasted_iota(jnp.int32, (tm, ew), 1) % FOX_EXTRA
    eq_all = jnp.where(lane == 0, hi, jnp.where(lane == 1, mid, jnp.where(lane == 2, lo,
                       jnp.where(lane < 6, 1.0, 0.0))))
    ek_all = jnp.where(lane < 3, 1.0, jnp.where(lane == 3, -hi, jnp.where(lane == 4, -mid,
                       jnp.where(lane == 5, -lo, 0.0))))
    for hd in range(N_HEADS):
        hs = slice(hd * HEAD_DIM, (hd + 1) * HEAD_DIM)
        es = slice(hd * FOX_EXTRA, (hd + 1) * FOX_EXTRA)
        fq_ref[0, hd] = jnp.concatenate([fq[:, hs], eq_all[:, es]], axis=-1).astype(BF16)
        fk_ref[0, hd] = jnp.concatenate([fk[:, hs], ek_all[:, es]], axis=-1).astype(BF16)

    fvt = lax.dot_general(wvt_ref[...], hb, (((1,), (1,)), ((), ())), preferred_element_type=F32)
    for hd in range(N_HEADS):
        fvt_ref[0, hd, 0:HEAD_DIM, :] = fvt[hd * HEAD_DIM:(hd + 1) * HEAD_DIM, :].astype(BF16)
        fvt_ref[0, hd, HEAD_DIM:VT_ROWS, :] = _ones_rows(tm)

    lat = mm(C_LAT, C_GATE)
    kvn = _rms(lat[:, MLA_LORA:], gkv_ref[...]).astype(BF16)

    def gate_chunk(i):
        c0 = C_GATE + i * WIDTH
        gate_ref[0, :, i * WIDTH:(i + 1) * WIDTH] = _sigmoid(mm(c0, c0 + WIDTH)).astype(BF16)

    assert N_BRANCH * D_MODEL // WIDTH == 6
    gate_chunk(0)
    _mla_q(lat[:, :MLA_LORA], cos, sin, gq_ref, wq_ref, mq_ref)
    gate_chunk(1)
    gate_chunk(2)
    _mla_k(kvn, small[:, S_KR:S_KR + MLA_ROPE], cos, sin, wk_ref, mk_ref)
    gate_chunk(3)
    gate_chunk(4)
    _mla_v(kvn, mwvt_ref, mvt_ref)
    gate_chunk(5)


def _proj(x, mod, gain, w_cat, w_fv_t, mu, b_f, positions, gq, gkv, wq, wk, wv_t, freq, *, tm=256):
    B, T, D = x.shape
    nt = T // tm
    row3 = lambda b, j: (b, j, 0)
    hm = lambda d: (jax.ShapeDtypeStruct((B, N_HEADS, T, d), BF16),
                    pl.BlockSpec((1, N_HEADS, tm, d), lambda b, j: (b, 0, j, 0)))
    vt = (jax.ShapeDtypeStruct((B, N_HEADS, VT_ROWS, T), BF16),
          pl.BlockSpec((1, N_HEADS, VT_ROWS, tm), lambda b, j: (b, 0, 0, j)))
    outs = [(jax.ShapeDtypeStruct((B, T, 3 * WIDTH), BF16), pl.BlockSpec((1, tm, 3 * WIDTH), row3)),
            (jax.ShapeDtypeStruct((B, T, SMALL_W), F32), pl.BlockSpec((1, tm, SMALL_W), row3)),
            hm(FOX_QK), hm(FOX_QK), vt,
            (jax.ShapeDtypeStruct((B, T, N_BRANCH * D), BF16),
             pl.BlockSpec((1, tm, N_BRANCH * D), row3)),
            hm(MLA_QK), hm(MLA_QK), vt]
    consts = [gain, w_cat, w_fv_t, mu, b_f]
    mla_consts = [gq, gkv, wq, wk, wv_t, freq]
    return pl.pallas_call(
        functools.partial(_proj_body, tm=tm),
        out_shape=tuple(o[0] for o in outs),
        grid=(B, nt),
        in_specs=([pl.BlockSpec((1, tm, D), row3), pl.BlockSpec((1, 6, D), lambda b, j: (b, 0, 0))]
                  + [_const_spec(t.shape) for t in consts]
                  + [pl.BlockSpec((1, tm, 1), row3)]
                  + [_const_spec(t.shape) for t in mla_consts]),
        out_specs=tuple(o[1] for o in outs),
        scratch_shapes=[pltpu.VMEM((8, N_SHIFT), F32),
                        pltpu.VMEM((8, 128), F32)],
        compiler_params=_cparams(("arbitrary", "arbitrary")),
        name="proj",
    )(x, mod, *consts, positions, *mla_consts)


def _rwkv_body(*refs, has_vres, n_chunks):
    if has_vres:
        (rkv_ref, small_ref, vfirst_ref, lw_ref, la_ref, lg_ref, lv_ref, vec_ref, ones_ref,
         y_ref, ht_ref) = refs
    else:
        rkv_ref, small_ref, lw_ref, la_ref, lg_ref, vec_ref, ones_ref, y_ref, ht_ref = refs
    C = RW_CHUNK
    TC = n_chunks * C
    j = pl.program_id(1)

    @pl.when(j == 0)
    def _():
        ht_ref[...] = jnp.zeros_like(ht_ref)

    def vec(i):
        return vec_ref[i:i + 1, :]

    w0, a0, k_k, k_a, r_k, gn_w, gn_b, v0 = (vec(i) for i in range(8))
    ones_bd = ones_ref[...]
    HW = ones_bd.shape[0]

    def headsum(t):
        tb = t.astype(BF16)
        return jnp.concatenate([jnp.dot(tb[:, c0:c0 + HW], ones_bd, preferred_element_type=F32)
                                for c0 in range(0, WIDTH, HW)], axis=1)

    def lora(t, w_ref):
        return jnp.dot(t.astype(BF16), w_ref[...], preferred_element_type=F32)

    r = rkv_ref[0, :, 0:WIDTH].astype(F32)
    k = rkv_ref[0, :, WIDTH:2 * WIDTH].astype(F32)
    v = rkv_ref[0, :, 2 * WIDTH:3 * WIDTH].astype(F32)
    sm = small_ref[0]
    wl = sm[:, S_WL:S_WL + RW_DECAY_LORA]
    al = sm[:, S_AL:S_AL + RW_A_LORA]
    gl = sm[:, S_GL:S_GL + RW_G_LORA]

    w = -_softplus(-(w0 + lora(jnp.tanh(wl), lw_ref))) - 0.5
    logd = -jnp.exp(w)
    a = _sigmoid(a0 + lora(al, la_ref))
    g = lora(_sigmoid(gl), lg_ref)
    if has_vres:
        lvv = sm[:, S_VR:S_VR + RW_V_LORA]
        v = v + (vfirst_ref[0].astype(F32) - v) * _sigmoid(v0 + lora(lvv, lv_ref))
    kk = k * k_k
    kk = kk * lax.rsqrt(jnp.maximum(headsum(kk * kk), 1e-24))
    k2 = k * (1.0 + (a - 1.0) * k_a)

    ri = lax.broadcasted_iota(jnp.int32, (TC, TC), 0)
    ci = lax.broadcasted_iota(jnp.int32, (TC, TC), 1)
    tri = jnp.where((ci <= ri) & ((ri // C) == (ci // C)), 1.0, 0.0).astype(BF16)
    ld_hi, ld_mid, ld_lo = _split3(logd)
    cl = (jnp.dot(tri, ld_hi, preferred_element_type=F32) + jnp.dot(tri, ld_mid, preferred_element_type=F32)
          + jnp.dot(tri, ld_lo, preferred_element_type=F32))
    cl_last = jnp.concatenate([jnp.broadcast_to(cl[(c + 1) * C - 1:(c + 1) * C, :], (C, WIDTH))
                               for c in range(n_chunks)], axis=0)
    e_neg = jnp.exp(-cl)
    e_end = jnp.exp(cl_last - cl)
    ka = kk * a
    a_t = kk * jnp.exp(cl - logd)
    r_t = r * jnp.exp(cl)
    b_m = -ka * e_neg
    k_m = k2 * e_neg
    b_p = -ka * e_end
    k_p = k2 * e_end
    p_c = jnp.exp(cl_last)

    bi = lax.broadcasted_iota(jnp.int32, (GW, GW), 0)
    bj = lax.broadcasted_iota(jnp.int32, (GW, GW), 1)
    bd_mask = (bi // HEAD_DIM) == (bj // HEAD_DIM)
    strict = bi > bj
    incl = bi >= bj
    eye = (bi == bj).astype(F32)

    def bd(t):
        return jnp.where(bd_mask, jnp.concatenate([t] * RW_GROUP, axis=0), 0.0)

    def collapse(t):
        return sum(t[hh * C:(hh + 1) * C, :] for hh in range(RW_GROUP))

    def dot16(p, q):
        return jnp.dot(p.astype(BF16), q.astype(BF16), preferred_element_type=F32)

    chains = [(ch, gi) for ch in range(n_chunks) for gi in range(N_HEADS // RW_GROUP)]
    nc = len(chains)

    def piece(t, c):
        ch, gi = chains[c]
        return t[ch * C:(ch + 1) * C, gi * GW:(gi + 1) * GW]

    def bds(t):
        return [bd(piece(t, c)).astype(BF16) for c in range(nc)]

    at_bd, rt_bd, bm_bd, km_bd, bp_bd, kp_bd, v_bd = (bds(t) for t in (a_t, r_t, b_m, k_m, b_p, k_p, v))
    a_all = [lax.dot_general(jnp.concatenate([at_bd[c], rt_bd[c]], axis=0),
                             jnp.concatenate([bm_bd[c], km_bd[c]], axis=0),
                             (((1,), (1,)), ((), ())), preferred_element_type=F32) for c in range(nc)]
    a_ab = [jnp.where(strict, a_all[c][:GW, :GW], 0.0) for c in range(nc)]
    a_ak = [jnp.where(strict, a_all[c][:GW, GW:], 0.0).astype(BF16) for c in range(nc)]
    m_rb = [jnp.where(incl, a_all[c][GW:, :GW], 0.0).astype(BF16) for c in range(nc)]
    m_rk = [jnp.where(incl, a_all[c][GW:, GW:], 0.0).astype(BF16) for c in range(nc)]

    pw = [t.astype(BF16) for t in a_ab]
    tinv = [eye + t for t in a_ab]
    n_sq = int(np.log2(C)) - 1
    for it in range(n_sq):
        pw = [jnp.dot(pw[c], pw[c], preferred_element_type=F32).astype(BF16) for c in range(nc)]
        tinv = [tinv[c] + dot16(tinv[c], pw[c]) for c in range(nc)]
    tinv = [t.astype(BF16) for t in tinv]

    akv = [dot16(a_ak[c], v_bd[c]) for c in range(nc)]
    x = [dot16(tinv[c], jnp.concatenate([at_bd[c], akv[c].astype(BF16)], axis=1)) for c in range(nc)]
    wm_bd = [t[:, :GW].astype(BF16) for t in x]
    u0_bd = [t[:, GW:].astype(BF16) for t in x]
    y1_bd = [dot16(jnp.concatenate([m_rk[c], m_rb[c]], axis=1),
                   jnp.concatenate([v_bd[c], u0_bd[c]], axis=0)) for c in range(nc)]
    rm_bd = [dot16(m_rb[c], wm_bd[c]) for c in range(nc)]
    qt = [dot16(x[c][:, :GW].T, bp_bd[c]) for c in range(nc)]
    h1t = [dot16(jnp.concatenate([x[c][:, GW:], v_bd[c].astype(F32)], axis=0).T,
                 jnp.concatenate([bp_bd[c], kp_bd[c]], axis=0)) for c in range(nc)]
    local = {}
    for c, (ch, gi) in enumerate(chains):
        local[ch, gi] = (collapse(y1_bd[c]), (piece(r_t, c) + collapse(rm_bd[c])).astype(BF16),
                         qt[c].astype(BF16), h1t[c], p_c[ch * C:ch * C + 1, gi * GW:(gi + 1) * GW])

    y_rows = []
    for ch in range(n_chunks):
        ys = []
        for gi in range(N_HEADS // RW_GROUP):
            y1, rm, qt, h1t, pc = local[ch, gi]
            ht = ht_ref[gi]
            htb = ht.astype(BF16)
            ys.append(y1 + lax.dot_general(rm, htb, (((1,), (1,)), ((), ())), preferred_element_type=F32))
            ht_ref[gi] = ht * pc + jnp.dot(htb, qt, preferred_element_type=F32) + h1t
        y_rows.append(jnp.concatenate(ys, axis=1))
    y = jnp.concatenate(y_rows, axis=0)
    inv_n = 1.0 / HEAD_DIM
    mean = headsum(y) * inv_n
    dlt = y - mean
    var = headsum(dlt * dlt) * inv_n
    y = dlt * lax.rsqrt(var + RW_GN_EPS) * gn_w + gn_b
    y = y + headsum(r * k2 * r_k) * v
    y_ref[0] = (y * g).astype(BF16)


def _rwkv(rkv, small, vfirst_rkv, lw, la, lg, lv, vecs, ones_bd, *, n_chunks=4):
    B, T, _ = rkv.shape
    C = n_chunks * RW_CHUNK
    has_vres = vfirst_rkv is not None
    row3 = lambda b, j: (b, j, 0)
    in_specs = [pl.BlockSpec((1, C, 3 * WIDTH), row3), pl.BlockSpec((1, C, SMALL_W), row3)]
    args = [rkv, small]
    if has_vres:
        in_specs.append(pl.BlockSpec((1, C, WIDTH), lambda b, j: (b, j, 2)))
        args.append(vfirst_rkv)
    w_list = [lw, la, lg] + ([lv] if has_vres else []) + [vecs, ones_bd]
    in_specs += [_const_spec(t.shape) for t in w_list]
    args += w_list
    return pl.pallas_call(
        functools.partial(_rwkv_body, has_vres=has_vres, n_chunks=n_chunks),
        out_shape=jax.ShapeDtypeStruct((B, T, WIDTH), BF16),
        grid=(B, T // C),
        in_specs=in_specs,
        out_specs=pl.BlockSpec((1, C, WIDTH), row3),
        scratch_shapes=[pltpu.VMEM((N_HEADS // RW_GROUP, GW, GW), F32)],
        compiler_params=_cparams(("arbitrary", "arbitrary")),
        name="rwkv",
    )(*args)


def _attn_body(q_ref, k_ref, vt_ref, o_ref, m_ref, acc_ref, *, tq, tks):
    qi = pl.program_id(2)
    tk = tq
    hps = q_ref.shape[1]

    m_ref[...] = jnp.full_like(m_ref, NEG_BIG)
    acc_ref[...] = jnp.zeros_like(acc_ref)

    def step(ki, diag):
        k0 = pl.multiple_of(ki * tk, tk)
        nsub = tk // tks
        units = [(i, s) for s in range(nsub) for i in range(hps)]

        def lo_of(s):
            return s * tks if diag else 0

        def qk(i, s):
            kb = k_ref[0, i, pl.ds(k0 + s * tks, tks), :]
            st = lax.dot_general(kb, q_ref[0, i, lo_of(s):, :], (((1,), (1,)), ((), ())),
                                 preferred_element_type=F32)
            if diag:
                key = lax.broadcasted_iota(jnp.int32, st.shape, 0)
                qry = lax.broadcasted_iota(jnp.int32, st.shape, 1)
                st = jnp.where(key <= qry, st, NEG_BIG)
            return st

        def softmax(i, s, st):
            lo = lo_of(s)
            m_old = m_ref[i, :, lo:]
            m_new = jnp.maximum(m_old, jnp.max(st, axis=0, keepdims=True))
            m_ref[i, :, lo:] = m_new
            return jnp.exp2(st - m_new).astype(BF16), jnp.exp2(m_old - m_new)

        def pv(i, s, p, alpha):
            lo = lo_of(s)
            vtb = vt_ref[0, i, :, pl.ds(k0 + s * tks, tks)]
            acc_ref[i, :, lo:] = alpha * acc_ref[i, :, lo:] + jnp.dot(vtb, p, preferred_element_type=F32)

        n = len(units)
        sc = {u: qk(*units[u]) for u in range(min(2, n))}
        pr = {}
        for u in range(n):
            pr[u] = softmax(*units[u], sc.pop(u))
            if u + 2 < n:
                sc[u + 2] = qk(*units[u + 2])
            if u >= 1:
                pv(*units[u - 1], *pr.pop(u - 1))
        pv(*units[n - 1], *pr.pop(n - 1))

    def loop_body(ki, carry):
        step(ki, False)
        return carry

    lax.fori_loop(0, qi, loop_body, 0)
    step(qi, True)

    out_t = jnp.concatenate([acc_ref[i, 0:HEAD_DIM, :] / acc_ref[i, HEAD_DIM:HEAD_DIM + 1, :]
                             for i in range(hps)], axis=0)
    o_ref[0] = out_t.T.astype(BF16)


def _attn(q, k, vt, *, name, tq=512, tks=256, hps=8):
    B, H, T, dk = q.shape
    nq = T // tq
    return pl.pallas_call(
        functools.partial(_attn_body, tq=tq, tks=tks),
        out_shape=jax.ShapeDtypeStruct((B, T, H * HEAD_DIM), BF16),
        grid=(B, H // hps, nq),
        in_specs=[pl.BlockSpec((1, hps, tq, dk), lambda b, p, i: (b, p, i, 0)),
                  pl.BlockSpec((1, hps, T, dk), lambda b, p, i: (b, p, 0, 0)),
                  pl.BlockSpec((1, hps, VT_ROWS, T), lambda b, p, i: (b, p, 0, 0))],
        out_specs=pl.BlockSpec((1, tq, hps * HEAD_DIM), lambda b, p, i: (b, i, p)),
        scratch_shapes=[pltpu.VMEM((hps, 1, tq), F32), pltpu.VMEM((hps, VT_ROWS, tq), F32)],
        compiler_params=_cparams(("arbitrary", "arbitrary", "arbitrary")),
        name=name,
    )(q, k, vt)


def _merge_body(yr_ref, yf_ref, ym_ref, gate_ref, x_ref, mod_ref, wb_ref, wo_ref, o_ref):
    D = D_MODEL
    mixed = None
    for gi, y_ref in enumerate((yr_ref, yf_ref, ym_ref)):
        bd = jnp.dot(y_ref[0], wb_ref[gi], preferred_element_type=F32)
        term = gate_ref[0, :, gi * D:(gi + 1) * D].astype(F32) * bd
        mixed = term if mixed is None else mixed + term
    out = jnp.dot(mixed.astype(BF16), wo_ref[...], preferred_element_type=F32)
    o_ref[0] = x_ref[0] + mod_ref[0, 2:3, :] * out


def _merge(y_rw, y_fox, y_mla, gates, x, mod, wb, wo, *, tm=1024):
    B, T, D = x.shape
    row3 = lambda b, j: (b, j, 0)
    yspec = pl.BlockSpec((1, tm, WIDTH), row3)
    return pl.pallas_call(
        _merge_body,
        out_shape=jax.ShapeDtypeStruct((B, T, D), F32),
        grid=(B, T // tm),
        in_specs=[yspec, yspec, yspec,
                  pl.BlockSpec((1, tm, N_BRANCH * D), row3),
                  pl.BlockSpec((1, tm, D), row3),
                  pl.BlockSpec((1, 6, D), lambda b, j: (b, 0, 0)),
                  _const_spec(wb.shape), _const_spec(wo.shape)],
        out_specs=pl.BlockSpec((1, tm, D), row3),
        compiler_params=_cparams(("arbitrary", "arbitrary")),
        name="merge",
    )(y_rw, y_fox, y_mla, gates, x, mod, wb, wo)


def _ffn_body(x_ref, mod_ref, g_ref, wup_ref, cw_ref, cb_ref, wdn_ref, fg_ref, o_ref, carry_ref,
              *, tm, fc, final):
    j = pl.program_id(1)

    @pl.when(j == 0)
    def _():
        carry_ref[...] = jnp.zeros_like(carry_ref)

    x = x_ref[0]
    hb = _modulated_norm(x, g_ref[...], mod_ref[0, 3:4, :], mod_ref[0, 4:5, :]).astype(BF16)
    row8 = lax.broadcasted_iota(jnp.int32, (8, 1), 0)

    def up(c0):
        return jnp.dot(hb, wup_ref[:, c0:c0 + fc], preferred_element_type=F32)

    def conv(u, c0):
        t2 = carry_ref[0:1, c0:c0 + fc]
        t1 = carry_ref[1:2, c0:c0 + fc]
        carry_ref[0:2, c0:c0 + fc] = u[tm - 2:tm, :]
        r1 = pltpu.roll(u, 1, 0)
        r2 = pltpu.roll(u, 2, 0)
        p1 = jnp.concatenate([jnp.where(row8 == 0, t1, r1[:8]), r1[8:]], axis=0)
        p2 = jnp.concatenate([jnp.where(row8 == 0, t2, jnp.where(row8 == 1, t1, r2[:8])), r2[8:]], axis=0)
        return (u * cw_ref[2:3, c0:c0 + fc] + p1 * cw_ref[1:2, c0:c0 + fc]
                + p2 * cw_ref[0:1, c0:c0 + fc] + cb_ref[:, c0:c0 + fc])

    n_ff = D_FF // fc
    raw = {0: (up(0), up(D_FF))}
    acts = {}
    acc = None

    dg = 4

    def down(c0, c1):
        a = jnp.concatenate([acts.pop(c) for c in range(c0, c1)], axis=1)
        part = jnp.dot(a, wdn_ref[c0 * fc:c1 * fc, :], preferred_element_type=F32)
        return part if acc is None else acc + part

    for c in range(n_ff):
        rg, rv = raw.pop(c)
        if c + 1 < n_ff:
            nxt_g = up((c + 1) * fc)
        ug = conv(rg, c * fc)
        gate = ug * _sigmoid(ug)
        if c + 1 < n_ff:
            raw[c + 1] = (nxt_g, up(D_FF + (c + 1) * fc))
        uv = conv(rv, D_FF + c * fc)
        acts[c] = (gate * uv).astype(BF16)
        if c >= dg and c % dg == 0:
            acc = down(c - dg, c)
    acc = down((n_ff - 1) // dg * dg, n_ff)
    out = x + mod_ref[0, 5:6, :] * acc
    if final:
        out = out * lax.rsqrt(jnp.mean(out * out, axis=-1, keepdims=True) + NORM_EPS) * fg_ref[...]
    o_ref[0] = out


def _ffn(x, mod, gain, w_up, conv_w, conv_b, w_down, final_g, *, final, tm=512, fc=256):
    B, T, D = x.shape
    row3 = lambda b, j: (b, j, 0)
    return pl.pallas_call(
        functools.partial(_ffn_body, tm=tm, fc=fc, final=final),
        out_shape=jax.ShapeDtypeStruct((B, T, D), F32),
        grid=(B, T // tm),
        in_specs=[pl.BlockSpec((1, tm, D), row3),
                  pl.BlockSpec((1, 6, D), lambda b, j: (b, 0, 0)),
                  _const_spec(gain.shape), _const_spec(w_up.shape), _const_spec(conv_w.shape),
                  _const_spec(conv_b.shape), _const_spec(w_down.shape), _const_spec(final_g.shape)],
        out_specs=pl.BlockSpec((1, tm, D), row3),
        scratch_shapes=[pltpu.VMEM((8, 2 * D_FF), F32)],
        compiler_params=_cparams(("arbitrary", "arbitrary")),
        name="ffn",
    )(x, mod, gain, w_up, conv_w, conv_b, w_down, final_g)


def _cat_weight(w_in_i, vres_down_i):
    D = w_in_i.shape[0]
    o_wl, o_k, o_v, o_al, o_gl, rw_cols = 512, 576, 1088, 1600, 1664, 1792
    o_fox = rw_cols
    o_ff = o_fox + 3 * WIDTH
    o_mq = o_ff + N_HEADS
    o_kr = o_mq + 2 * MLA_LORA
    o_gate = o_kr + MLA_ROPE
    vres = jnp.zeros((D, RW_V_LORA), w_in_i.dtype) if vres_down_i is None else vres_down_i
    pad = jnp.zeros((D, SMALL_W - (S_FF + N_HEADS)), w_in_i.dtype)
    cols = [w_in_i[:, 0:512], w_in_i[:, o_k:o_k + 512], w_in_i[:, o_v:o_v + 512],
            w_in_i[:, o_wl:o_wl + 64], w_in_i[:, o_al:o_al + 64], w_in_i[:, o_gl:o_gl + 128],
            vres, w_in_i[:, o_kr:o_kr + MLA_ROPE], w_in_i[:, o_ff:o_ff + N_HEADS], pad,
            w_in_i[:, o_fox:o_fox + 2 * WIDTH],
            w_in_i[:, o_mq:o_mq + 2 * MLA_LORA],
            w_in_i[:, o_gate:o_gate + N_BRANCH * D_MODEL]]
    w_fv_t = w_in_i[:, o_fox + 2 * WIDTH:o_fox + 3 * WIDTH].T
    return jnp.concatenate(cols, axis=1).astype(BF16), w_fv_t.astype(BF16)


def _cat_mu(mu_i, vres_mu_i):
    o_wl, o_k, o_v, o_al, o_gl = 512, 576, 1088, 1600, 1664
    vres = jnp.zeros((RW_V_LORA,), F32) if vres_mu_i is None else vres_mu_i
    tail = jnp.zeros((SMALL_W - S_KR,), F32)
    return jnp.concatenate([mu_i[0:512], mu_i[o_k:o_k + 512], mu_i[o_v:o_v + 512],
                            mu_i[o_wl:o_wl + 64], mu_i[o_al:o_al + 64], mu_i[o_gl:o_gl + 128],
                            vres, tail]).reshape(1, N_SHIFT)


def _mla_weights(w_q_up, w_kv_up):
    half = MLA_ROPE // 2
    wq = w_q_up.reshape(MLA_LORA, N_HEADS, MLA_QK)
    wq = jnp.concatenate([wq[:, :, :MLA_NOPE].reshape(MLA_LORA, -1),
                          wq[:, :, MLA_NOPE:MLA_NOPE + half].reshape(MLA_LORA, -1),
                          wq[:, :, MLA_NOPE + half:].reshape(MLA_LORA, -1)], axis=1)
    wkv = w_kv_up.reshape(MLA_LORA, N_HEADS, MLA_NOPE + HEAD_DIM)
    wk = wkv[:, :, :MLA_NOPE].reshape(MLA_LORA, -1)
    wv_t = wkv[:, :, MLA_NOPE:].reshape(MLA_LORA, -1).T
    return wq.astype(BF16), wk.astype(BF16), wv_t.astype(BF16)


def kernel(x, c, positions, norm1_g, norm2_g, w_ada, b_ada, w_in, mu_shift, rw_w_up, rw_w0, rw_a_up, rw_a0, rw_g_up, rw_k_k, rw_k_a, rw_r_k, rw_gn_w, rw_gn_b, rw_vres_down, rw_vres_mu, rw_vres_up, rw_v0, fox_b_f, mla_q_norm_g, mla_w_q_up, mla_kv_norm_g, mla_w_kv_up, w_branch, w_o, ffn_w_up, ffn_conv_w, ffn_conv_b, ffn_w_down, final_g):
    B, T, D = x.shape
    depth = w_in.shape[0]
    half = MLA_ROPE // 2
    inv_freq = np.power(np.float32(ROPE_BASE), -np.arange(half, dtype=np.float32) / np.float32(half))
    freq = jnp.asarray(np.tile(inv_freq.astype(np.float32), 128 // half).reshape(1, 128))
    hid = np.arange(WIDTH // 2) // HEAD_DIM
    ones_bd = jnp.asarray((hid[:, None] == hid[None, :]).astype(np.float32), BF16)
    pos3 = positions.reshape(B, T, 1)

    mods = _ada(c, w_ada, b_ada).reshape(depth, B, 6, D)
    rkv0 = None
    for i in range(depth):
        mod = mods[i]
        w_cat, w_fv_t = _cat_weight(w_in[i], None if i == 0 else rw_vres_down[i - 1])
        mu = _cat_mu(mu_shift[i], None if i == 0 else rw_vres_mu[i - 1])
        wq, wk, wv_t = _mla_weights(mla_w_q_up[i], mla_w_kv_up[i])
        rkv, small, fq, fk, fvt, gates, mq, mk, mvt = _proj(
            x, mod, norm1_g[i].reshape(1, D), w_cat, w_fv_t, mu, fox_b_f[i].reshape(1, N_HEADS),
            pos3, mla_q_norm_g[i].reshape(1, -1), mla_kv_norm_g[i].reshape(1, -1), wq, wk, wv_t, freq)

        zero = jnp.zeros((WIDTH,), F32)
        vecs = jnp.stack([rw_w0[i], rw_a0[i], rw_k_k[i], rw_k_a[i], rw_r_k[i], rw_gn_w[i], rw_gn_b[i],
                          zero if i == 0 else rw_v0[i - 1]])
        y_rw = _rwkv(rkv, small, None if i == 0 else rkv0,
                     rw_w_up[i].astype(BF16), rw_a_up[i].astype(BF16), rw_g_up[i].astype(BF16),
                     None if i == 0 else rw_vres_up[i - 1].astype(BF16), vecs, ones_bd)
        if i == 0:
            rkv0 = rkv

        y_fox = _attn(fq, fk, fvt, name="attn_fox")

        y_mla = _attn(mq, mk, mvt, name="attn_mla")

        x = _merge(y_rw, y_fox, y_mla, gates, x, mod,
                   w_branch[i].astype(BF16), w_o[i].astype(BF16))
        x = _ffn(x, mod, norm2_g[i].reshape(1, D), ffn_w_up[i].astype(BF16), ffn_conv_w[i],
                 ffn_conv_b[i].reshape(1, -1), ffn_w_down[i].astype(BF16), final_g.reshape(1, D),
                 final=(i == depth - 1))
    return x
```
